```python
import jax, jax.numpy as jnp
from jax import lax
import numpy as np

D_MODEL = 4096
BATCH = 1
SEQ = 16384
DEPTH = 2
DEC_BATCH = 8
DEC_SEQ = 32
PAST_LEN = 2048

CHUNK = 64
N_META = 16
Q_BLOCK = 128
NORM_EPS = 1e-6
N_BRANCH = 3

SB_HEADS = 8
SB_DIM = 128
SB_W = SB_HEADS * SB_DIM
RW_HEADS = 16
RW_DIM = 64
RW_W = RW_HEADS * RW_DIM
RW_DECAY_LORA = 64
RW_AAA_LORA = 64
RW_GATE_LORA = 160
RW_COLS = 3 * RW_W + RW_DECAY_LORA + RW_AAA_LORA + RW_GATE_LORA
RW_GN_EPS = 64e-5
GDN_HEADS = 8
GDN_DIM = 128
GDN_W = GDN_HEADS * GDN_DIM
GDN_CONV = 4
BRANCH_W = SB_W
PEER_HEADS = 8
PEER_KEYS = 64
PEER_EXPERTS = PEER_KEYS * PEER_KEYS
PEER_QDIM = 256
PEER_TOPK = 16

OFF_SB = 0
OFF_RW = OFF_SB + 3 * SB_W
OFF_GDN = OFF_RW + RW_COLS
OFF_GDN_AB = OFF_GDN + 3 * GDN_W
OFF_GDN_Z = OFF_GDN_AB + 2 * GDN_HEADS
OFF_GATE = OFF_GDN_Z + GDN_W
P_IN = OFF_GATE + N_BRANCH * D_MODEL

F32 = jnp.float32

kernel_name = 'hybrid_stream_sb_rwkv7_gdn_peer_step'


def rmsnorm(x, g):
    xf = x.astype(F32)
    y = xf * lax.rsqrt(jnp.mean(xf * xf, axis=-1, keepdims=True) + NORM_EPS)
    return (y * g.astype(F32)).astype(x.dtype)


def l2norm(x, eps=1e-6):
    return x * lax.rsqrt(jnp.sum(x * x, axis=-1, keepdims=True) + eps)


def causal_conv(u, buf, w):
    width = w.shape[0]
    t_len = u.shape[1]
    full = jnp.concatenate([buf.astype(u.dtype), u], axis=1)
    out = full[:, 0:t_len] * w[0]
    for i in range(1, width):
        out = out + full[:, i:i + t_len] * w[i]
    return out, full[:, t_len:]


def sb_block(q, k, v, q_pos, k_pos):
    b, h, nq, _ = q.shape
    nk = k.shape[2]
    nkb = nk // Q_BLOCK
    z = jnp.einsum('bhqd,bhkd->bhqk', q, k) * (SB_DIM ** -0.5)
    mask = k_pos[None, :] < q_pos[:, None]
    log_fail = jnp.where(mask, -jax.nn.softplus(z), 0.0)
    lf = log_fail.reshape(b, h, nq, nkb, Q_BLOCK)
    idx = jnp.arange(Q_BLOCK)
    tri_in = (idx[:, None] >= idx[None, :]).astype(F32)
    rev = jnp.einsum('bhqnj,js->bhqns', lf, tri_in)
    blk = jnp.arange(nkb)
    tri_out = (blk[:, None] > blk[None, :]).astype(F32)
    later = jnp.einsum('bhqm,mn->bhqn', rev[..., 0], tri_out)
    log_att = z + (rev + later[..., None]).reshape(b, h, nq, nk)
    att = jnp.exp(jnp.where(mask, log_att, -jnp.inf))
    return jnp.einsum('bhqk,bhkd->bhqd', att, v)


def sb_attention_prompt(q, k, v):
    b, t, h, d = q.shape
    tp = -(-t // Q_BLOCK) * Q_BLOCK
    nb = tp // Q_BLOCK

    def heads_first(a):
        return jnp.pad(a.astype(F32), ((0, 0), (0, tp - t), (0, 0), (0, 0))).transpose(0, 2, 1, 3)

    qh, kh, vh = heads_first(q), heads_first(k), heads_first(v)
    pos = jnp.arange(tp)
    outs = []
    for i in range(nb):
        lo, hi = i * Q_BLOCK, (i + 1) * Q_BLOCK
        outs.append(sb_block(qh[:, :, lo:hi], kh[:, :, :hi], vh[:, :, :hi], pos[lo:hi], pos[:hi]))
    out = jnp.concatenate(outs, axis=2)
    return out.transpose(0, 2, 1, 3)[:, :t]


def sb_attention_cached(q, k, v, cache_k, cache_v):
    past = cache_k.shape[1]
    t = q.shape[1]
    kpad = (-(past + t)) % Q_BLOCK

    def keys(c, n):
        a = jnp.concatenate([c.astype(F32), n.astype(F32)], axis=1)
        return jnp.pad(a, ((0, 0), (0, kpad), (0, 0), (0, 0))).transpose(0, 2, 1, 3)

    out = sb_block(q.astype(F32).transpose(0, 2, 1, 3), keys(cache_k, k), keys(cache_v, v),
                   past + jnp.arange(t), jnp.arange(past + t + kpad))
    return out.transpose(0, 2, 1, 3)


def rwkv7_chunked(r, logd, k, v, a, b, h0, chunk):
    bsz, t, h, dk = r.shape
    n = t // chunk

    def blocks(x):
        return jnp.moveaxis(x.reshape(bsz, n, chunk, h, x.shape[-1]), (1, 3), (0, 2))

    rc, kc, vc, ac, bc, ld = [blocks(x) for x in (r, k, v, a, b, logd)]
    g_in = jnp.cumsum(ld, axis=3)
    g_ex = g_in - ld
    r_t = rc * jnp.exp(g_in)
    a_t = ac * jnp.exp(g_ex)
    b_h = bc * jnp.exp(-g_in)
    k_h = kc * jnp.exp(-g_in)
    g_last = g_in[..., -1:, :]
    b_end = bc * jnp.exp(g_last - g_in)
    k_end = kc * jnp.exp(g_last - g_in)
    g_tot = jnp.exp(g_last[..., 0, :])
    idx = jnp.arange(chunk)
    strict = idx[:, None] > idx[None, :]
    incl = idx[:, None] >= idx[None, :]

    def pair(x, y, m):
        return jnp.where(m, jnp.einsum('nbhid,nbhjd->nbhij', x, y), 0.0)

    a_ab, a_ak = pair(a_t, b_h, strict), pair(a_t, k_h, strict)
    a_rb, a_rk = pair(r_t, b_h, incl), pair(r_t, k_h, incl)
    rhs = jnp.concatenate([a_t, jnp.einsum('nbhij,nbhje->nbhie', a_ak, vc)], axis=-1)
    sol = lax.linalg.triangular_solve(jnp.eye(chunk, dtype=F32) - a_ab, rhs, left_side=True, lower=True,
                                      unit_diagonal=True)
    w_mat, u0 = sol[..., :dk], sol[..., dk:]
    o_intra = jnp.einsum('nbhij,nbhje->nbhie', a_rk, vc)

    def step(hs, inp):
        w_i, u0_i, r_i, a_rb_i, o_i, b_i, k_i, v_i, gt_i = inp
        u = u0_i + jnp.einsum('bhcd,bhde->bhce', w_i, hs)
        o = o_i + jnp.einsum('bhcd,bhde->bhce', r_i, hs) + jnp.einsum('bhij,bhje->bhie', a_rb_i, u)
        hs = (hs * gt_i[..., None] + jnp.einsum('bhcd,bhce->bhde', b_i, u)
              + jnp.einsum('bhcd,bhce->bhde', k_i, v_i))
        return hs, o

    hs, o = lax.scan(step, h0, (w_mat, u0, r_t, a_rb, o_intra, b_end, k_end, vc, g_tot))
    return jnp.moveaxis(o, (0, 2), (1, 3)).reshape(bsz, t, h, -1), hs


def rwkv7_time_mix(cols, shift_buf, state, lp, prompt):
    b, t, _ = cols.shape
    full = jnp.concatenate([shift_buf.astype(cols.dtype), cols], axis=1)
    prev = full[:, :t]
    xs = (cols + (prev - cols) * lp['rw_mu']).astype(F32)
    r, k, v, xw, xa, xg = jnp.split(
        xs, [RW_W, 2 * RW_W, 3 * RW_W, 3 * RW_W + RW_DECAY_LORA, 3 * RW_W + RW_DECAY_LORA + RW_AAA_LORA], axis=-1)
    log_w = -jax.nn.softplus(-(lp['rw_w0'] + jnp.tanh(xw) @ lp['rw_w2'])) - 0.5
    log_decay = -jnp.exp(log_w)
    a = jax.nn.sigmoid(lp['rw_a0'] + xa @ lp['rw_a2'])
    gate = jax.nn.sigmoid(xg) @ lp['rw_g2']

    def heads(u):
        return u.reshape(b, t, RW_HEADS, RW_DIM)

    r, k, v, log_decay, a = heads(r), heads(k), heads(v), heads(log_decay), heads(a)
    kk = k * lp['rw_k_k'].reshape(RW_HEADS, RW_DIM)
    kk = kk / jnp.maximum(jnp.sqrt(jnp.sum(kk * kk, axis=-1, keepdims=True)), 1e-12)
    k = k * (1.0 + (a - 1.0) * lp['rw_k_a'].reshape(RW_HEADS, RW_DIM))
    if prompt:
        chunk, back = CHUNK, (-t) % CHUNK
    else:
        chunk, back = t, 0

    def pad(u):
        return jnp.pad(u, ((0, 0), (0, back), (0, 0), (0, 0)))

    o, s_new = rwkv7_chunked(pad(r), pad(log_decay), pad(k), pad(v), pad(-kk), pad(kk * a),
                             state.astype(F32), chunk)
    o = o[:, :t]
    mu = jnp.mean(o, axis=-1, keepdims=True)
    var = jnp.mean(jnp.square(o - mu), axis=-1, keepdims=True)
    gn = ((o - mu) * lax.rsqrt(var + RW_GN_EPS)).reshape(b, t, RW_W) * lp['rw_ln_w'] + lp['rw_ln_b']
    bonus = (jnp.sum(r * k * lp['rw_r_k'], axis=-1, keepdims=True) * v).reshape(b, t, RW_W)
    return (gn + bonus) * gate, full[:, t:], s_new


def gdn_chunked(q, k, v, beta, g, s0, chunk):
    b, t, h, dk = q.shape
    dv = v.shape[-1]
    n = t // chunk

    def blocks(x):
        return jnp.moveaxis(x.reshape(b, n, chunk, h, x.shape[-1]), (1, 3), (0, 2))

    qc, kc, vc = blocks(q), blocks(k), blocks(v)
    bc = beta.reshape(b, n, chunk, h).transpose(1, 0, 3, 2)
    gc = jnp.cumsum(g.reshape(b, n, chunk, h).transpose(1, 0, 3, 2), axis=-1)
    idx = jnp.arange(chunk)
    incl = idx[:, None] >= idx[None, :]
    strict = idx[:, None] > idx[None, :]
    decay = jnp.exp(jnp.where(incl, gc[..., :, None] - gc[..., None, :], -jnp.inf))
    kb = kc * bc[..., None]
    m = jnp.where(strict, jnp.einsum('nbhid,nbhjd->nbhij', kb, kc) * decay, 0.0)
    lower = m + jnp.eye(chunk, dtype=F32)
    rhs = jnp.concatenate([vc * bc[..., None], kb * jnp.exp(gc)[..., None]], axis=-1)
    sol = lax.linalg.triangular_solve(lower, rhs, left_side=True, lower=True, unit_diagonal=True)
    u, w = sol[..., :dv], sol[..., dv:]
    attn = jnp.einsum('nbhid,nbhjd->nbhij', qc, kc) * decay
    q_dec = qc * jnp.exp(gc)[..., None]
    k_dec = kc * jnp.exp(gc[..., -1:] - gc)[..., None]
    g_last = jnp.exp(gc[..., -1])

    def step(s, inp):
        u_i, w_i, q_i, k_i, attn_i, gl_i = inp
        v_new = u_i - jnp.einsum('bhcd,bhde->bhce', w_i, s)
        o = jnp.einsum('bhcd,bhde->bhce', q_i, s) + jnp.einsum('bhij,bhje->bhie', attn_i, v_new)
        s = s * gl_i[..., None, None] + jnp.einsum('bhcd,bhce->bhde', k_i, v_new)
        return s, o

    s_new, o = lax.scan(step, s0, (u, w, q_dec, k_dec, attn, g_last))
    return jnp.moveaxis(o, (0, 2), (1, 3)).reshape(b, t, h, dv), s_new


def gated_deltanet(qkv, ab, z, conv_buf, state, lp, prompt):
    b, t, _ = qkv.shape
    qkv_c, conv_new = causal_conv(qkv, conv_buf, lp['gdn_conv_w'])
    qkv_c = jax.nn.silu(qkv_c.astype(F32))
    q, k, v = [u.reshape(b, t, GDN_HEADS, GDN_DIM) for u in jnp.split(qkv_c, 3, axis=-1)]
    q = l2norm(q) * (GDN_DIM ** -0.5)
    k = l2norm(k)
    abf = ab.astype(F32)
    beta = jax.nn.sigmoid(abf[..., :GDN_HEADS])
    g = -jnp.exp(lp['gdn_a_log'].astype(F32)) * jax.nn.softplus(abf[..., GDN_HEADS:] + lp['gdn_dt_bias'])
    if prompt:
        front = (-N_META) % CHUNK
        back = (-(front + t)) % CHUNK
        chunk = CHUNK
    else:
        front, back, chunk = 0, 0, t

    def pad(u):
        return jnp.pad(u, ((0, 0), (front, back)) + ((0, 0),) * (u.ndim - 2))

    o, s_new = gdn_chunked(pad(q), pad(k), pad(v), pad(beta), pad(g), state.astype(F32), chunk)
    o = o[:, front:front + t]
    o = o * lax.rsqrt(jnp.mean(o * o, axis=-1, keepdims=True) + NORM_EPS) * lp['gdn_norm_w']
    o = o.reshape(b, t, GDN_W) * jax.nn.silu(z.astype(F32))
    return o, conv_new, s_new


def peer_ffn(xn, lp):
    b, t, d = xn.shape
    n_tok = b * t
    x2 = xn.reshape(n_tok, d)
    q = (x2 @ lp['peer_wq']).astype(F32).reshape(n_tok, PEER_HEADS, 2, PEER_QDIM // 2)
    s = jnp.einsum('thpc,hpkc->thpk', q, lp['peer_subkeys'].astype(F32))
    sv, si = lax.top_k(s, PEER_TOPK)
    cand = sv[:, :, 0, :, None] + sv[:, :, 1, None, :]
    cidx = si[:, :, 0, :, None] * PEER_KEYS + si[:, :, 1, None, :]
    fv, fi = lax.top_k(cand.reshape(n_tok, PEER_HEADS, PEER_TOPK * PEER_TOPK), PEER_TOPK)
    eidx = jnp.take_along_axis(cidx.reshape(n_tok, PEER_HEADS, PEER_TOPK * PEER_TOPK), fi, axis=-1)
    gate = jax.nn.softmax(fv, axis=-1)
    g_dense = jnp.zeros((n_tok, PEER_EXPERTS), F32).at[jnp.arange(n_tok)[:, None, None], eidx].add(gate)
    hid = jax.nn.gelu((x2 @ lp['peer_u'].T).astype(F32), approximate=False)
    out = (g_dense * hid).astype(x2.dtype) @ lp['peer_v']
    return out.reshape(b, t, d)


def hybrid_layer(x, lp, sb_cache_k, sb_cache_v, rw_shift, rw_state, gdn_conv, gdn_state, prompt):
    b, t, _ = x.shape
    xn = rmsnorm(x, lp['norm_mix'])
    cols = xn @ lp['w_in']

    def heads_sb(u):
        return u.reshape(b, t, SB_HEADS, SB_DIM)

    q_sb = heads_sb(cols[..., OFF_SB:OFF_SB + SB_W])
    k_sb = heads_sb(cols[..., OFF_SB + SB_W:OFF_SB + 2 * SB_W])
    v_sb = heads_sb(cols[..., OFF_SB + 2 * SB_W:OFF_RW])
    if prompt:
        o_sb = sb_attention_prompt(q_sb, k_sb, v_sb)
    else:
        o_sb = sb_attention_cached(q_sb, k_sb, v_sb, sb_cache_k, sb_cache_v)
    o_sb = o_sb.reshape(b, t, SB_W)
    o_rw, rw_shift_new, rw_state_new = rwkv7_time_mix(cols[..., OFF_RW:OFF_GDN], rw_shift, rw_state, lp, prompt)
    o_gdn, gdn_conv_new, gdn_state_new = gated_deltanet(
        cols[..., OFF_GDN:OFF_GDN_AB], cols[..., OFF_GDN_AB:OFF_GDN_Z], cols[..., OFF_GDN_Z:OFF_GATE],
        gdn_conv, gdn_state, lp, prompt)
    merged = jnp.zeros_like(x)
    for i, o_b in enumerate((o_sb, o_rw, o_gdn)):
        gate = jax.nn.sigmoid(cols[..., OFF_GATE + i * D_MODEL:OFF_GATE + (i + 1) * D_MODEL])
        merged = merged + gate * (o_b.astype(x.dtype) @ lp['w_branch'][i])
    x = x + (merged @ lp['w_out']).astype(x.dtype)
    x = x + peer_ffn(rmsnorm(x, lp['norm_ffn']), lp).astype(x.dtype)
    return x, (k_sb, v_sb, rw_shift_new, rw_state_new, gdn_conv_new, gdn_state_new)


def setup_inputs(seed: int = 0) -> dict:
    key = jax.random.key(seed)
    ks = iter(jax.random.split(key, 48))

    def nrm(shape, scale):
        return jax.random.normal(next(ks), shape, F32) * scale

    def unif(shape, lo, hi):
        return jax.random.uniform(next(ks), shape, F32, lo, hi)

    L = DEPTH
    dt = jnp.exp(unif((L, GDN_HEADS), float(np.log(1e-3)), float(np.log(1e-1))))
    dt_bias = dt + jnp.log(-jnp.expm1(-dt))
    return {
        'x_prompt': nrm((BATCH, SEQ, D_MODEL), 1.0),
        'x_sample': nrm((DEC_BATCH, DEC_SEQ, D_MODEL), 1.0),
        'cache_sb_k': nrm((L, DEC_BATCH, PAST_LEN, SB_HEADS, SB_DIM), 1.0),
        'cache_sb_v': nrm((L, DEC_BATCH, PAST_LEN, SB_HEADS, SB_DIM), 1.0),
        'state_rw_shift': nrm((L, DEC_BATCH, 1, RW_COLS), 1.0),
        'state_rw': nrm((L, DEC_BATCH, RW_HEADS, RW_DIM, RW_DIM), 0.1),
        'state_gdn_conv': nrm((L, DEC_BATCH, GDN_CONV - 1, 3 * GDN_W), 1.0),
        'state_gdn': nrm((L, DEC_BATCH, GDN_HEADS, GDN_DIM, GDN_DIM), 0.1),
        'meta_tokens': nrm((N_META, D_MODEL), 1.0),
        'norm_mix': 1.0 + nrm((L, D_MODEL), 0.02),
        'w_in': nrm((L, D_MODEL, P_IN), D_MODEL ** -0.5),
        'rw_mu': unif((L, RW_COLS), 0.0, 1.0),
        'rw_w0': -1.0 + nrm((L, RW_W), 0.5),
        'rw_w2': nrm((L, RW_DECAY_LORA, RW_W), 0.5 * RW_DECAY_LORA ** -0.5),
        'rw_a0': nrm((L, RW_W), 0.1),
        'rw_a2': nrm((L, RW_AAA_LORA, RW_W), 0.5 * RW_AAA_LORA ** -0.5),
        'rw_g2': nrm((L, RW_GATE_LORA, RW_W), RW_GATE_LORA ** -0.5),
        'rw_k_k': 0.85 + nrm((L, RW_W), 0.05),
        'rw_k_a': 1.0 + nrm((L, RW_W), 0.05),
        'rw_r_k': nrm((L, RW_HEADS, RW_DIM), 0.1),
        'rw_ln_w': 1.0 + nrm((L, RW_W), 0.05),
        'rw_ln_b': nrm((L, RW_W), 0.01),
        'gdn_conv_w': nrm((L, GDN_CONV, 3 * GDN_W), GDN_CONV ** -0.5),
        'gdn_a_log': jnp.log(unif((L, GDN_HEADS), 1.0, 16.0)),
        'gdn_dt_bias': dt_bias,
        'gdn_norm_w': 1.0 + nrm((L, GDN_DIM), 0.02),
        'w_branch': nrm((L, N_BRANCH, BRANCH_W, D_MODEL), BRANCH_W ** -0.5),
        'w_out': nrm((L, D_MODEL, D_MODEL), D_MODEL ** -0.5),
        'norm_ffn': 1.0 + nrm((L, D_MODEL), 0.02),
        'peer_wq': nrm((L, D_MODEL, PEER_HEADS * PEER_QDIM), D_MODEL ** -0.5),
        'peer_subkeys': nrm((L, PEER_HEADS, 2, PEER_KEYS, PEER_QDIM // 2), (PEER_QDIM // 2) ** -0.5),
        'peer_u': nrm((L, PEER_EXPERTS, D_MODEL), D_MODEL ** -0.5),
        'peer_v': nrm((L, PEER_EXPERTS, D_MODEL), PEER_HEADS ** -0.5),
        'norm_final': 1.0 + nrm((D_MODEL,), 0.02),
    }


def reference(x_prompt, x_sample, cache_sb_k, cache_sb_v, state_rw_shift, state_rw, state_gdn_conv, state_gdn,
              meta_tokens, norm_mix, w_in, rw_mu, rw_w0, rw_w2, rw_a0, rw_a2, rw_g2, rw_k_k, rw_k_a, rw_r_k,
              rw_ln_w, rw_ln_b, gdn_conv_w, gdn_a_log, gdn_dt_bias, gdn_norm_w, w_branch, w_out, norm_ffn,
              peer_wq, peer_subkeys, peer_u, peer_v, norm_final):
    b_p = x_prompt.shape[0]
    hp = jnp.concatenate(
        [jnp.broadcast_to(meta_tokens.astype(x_prompt.dtype)[None], (b_p, N_META, D_MODEL)), x_prompt], axis=1)
    hs = x_sample
    p_rows, s_rows = [], []
    for l in range(DEPTH):
        lp = dict(norm_mix=norm_mix[l], w_in=w_in[l], rw_mu=rw_mu[l], rw_w0=rw_w0[l], rw_w2=rw_w2[l],
                  rw_a0=rw_a0[l], rw_a2=rw_a2[l], rw_g2=rw_g2[l], rw_k_k=rw_k_k[l], rw_k_a=rw_k_a[l],
                  rw_r_k=rw_r_k[l], rw_ln_w=rw_ln_w[l], rw_ln_b=rw_ln_b[l], gdn_conv_w=gdn_conv_w[l],
                  gdn_a_log=gdn_a_log[l], gdn_dt_bias=gdn_dt_bias[l], gdn_norm_w=gdn_norm_w[l],
                  w_branch=w_branch[l], w_out=w_out[l], norm_ffn=norm_ffn[l], peer_wq=peer_wq[l],
                  peer_subkeys=peer_subkeys[l], peer_u=peer_u[l], peer_v=peer_v[l])
        hp, p_new = hybrid_layer(
            hp, lp, None, None,
            jnp.zeros((b_p, 1, RW_COLS), hp.dtype),
            jnp.zeros((b_p, RW_HEADS, RW_DIM, RW_DIM), F32),
            jnp.zeros((b_p, GDN_CONV - 1, 3 * GDN_W), hp.dtype),
            jnp.zeros((b_p, GDN_HEADS, GDN_DIM, GDN_DIM), F32),
            True)
        hs, s_new = hybrid_layer(
            hs, lp, cache_sb_k[l], cache_sb_v[l], state_rw_shift[l], state_rw[l],
            state_gdn_conv[l], state_gdn[l], False)
        p_rows.append(p_new)
        s_rows.append(s_new)
    p_sb_k, p_sb_v, p_rw_shift, p_rw_state, p_gdn_conv, p_gdn_state = [jnp.stack(z) for z in zip(*p_rows)]
    s_sb_k, s_sb_v, s_rw_shift, s_rw_state, s_gdn_conv, s_gdn_state = [jnp.stack(z) for z in zip(*s_rows)]
    y_prompt = rmsnorm(hp, norm_final)[:, N_META:]
    y_sample = rmsnorm(hs, norm_final)
    return (y_prompt, y_sample, p_sb_k, p_sb_v, p_rw_shift, p_rw_state, p_gdn_conv, p_gdn_state,
            s_sb_k, s_sb_v, s_rw_shift, s_rw_state, s_gdn_conv, s_gdn_state)
```

```python
import functools

import jax
import jax.numpy as jnp
from jax import lax
from jax.experimental import pallas as pl
from jax.experimental.pallas import tpu as pltpu

F32 = jnp.float32
BF16 = jnp.bfloat16

V7X_VMEM_LIMIT_BYTES = 56 * 2**20
LANE = 128

NORM_EPS = 1e-6
RW_GN_EPS = 64e-5
CHUNK = 64
SB_DIM = 128
PEER_TOPK = 16
SB_LOG_UNDERFLOW = -104.0


def _round_up(n, m):
    return -(-n // m) * m


def _params(*sem):
    return pltpu.CompilerParams(dimension_semantics=sem, vmem_limit_bytes=V7X_VMEM_LIMIT_BYTES)


def _row_tile(rows):
    for t in (1280, 640, 256, 128, 64, 32, 16, 8):
        if rows % t == 0:
            return t
    raise ValueError(f"row count {rows} is not a multiple of 8")


def _col_tile(n):
    for t in (512, 384, 256, 128):
        if n % t == 0:
            return t
    raise ValueError(f"column count {n} is not a multiple of {LANE}")


def _rmsnorm_kernel(x_ref, g_ref, o_ref):
    x = x_ref[...]
    ms = jnp.mean(x * x, axis=-1, keepdims=True)
    o_ref[...] = (x * lax.rsqrt(ms + NORM_EPS) * g_ref[...]).astype(o_ref.dtype)


def _rmsnorm(x, g, out_dtype):
    rows, d = x.shape
    tr = min(_row_tile(rows), 256)
    return pl.pallas_call(
        _rmsnorm_kernel,
        grid=(rows // tr,),
        in_specs=[pl.BlockSpec((tr, d), lambda i: (i, 0)), pl.BlockSpec((1, d), lambda i: (0, 0))],
        out_specs=pl.BlockSpec((tr, d), lambda i: (i, 0)),
        out_shape=jax.ShapeDtypeStruct((rows, d), out_dtype),
        compiler_params=_params("parallel"),
        name="rmsnorm",
    )(x, g.reshape(1, d).astype(F32))


def _mm_kernel(a_ref, b_ref, *rest, nt, has_res):
    o_ref = rest[-1]
    dn = (((1,), (1,) if nt else (0,)), ((), ()))
    acc = lax.dot_general(a_ref[...], b_ref[...], dn, preferred_element_type=F32)
    if has_res:
        acc = acc + rest[0][...]
    o_ref[...] = acc.astype(o_ref.dtype)


def _mm(a, b, *, nt=False, res=None, out_dtype=F32):
    m, k = a.shape
    n = b.shape[0] if nt else b.shape[1]
    tm, tn = _row_tile(m), _col_tile(n)
    in_specs = [
        pl.BlockSpec((tm, k), lambda i, j: (i, 0)),
        pl.BlockSpec((tn, k), lambda i, j: (j, 0)) if nt else pl.BlockSpec((k, tn), lambda i, j: (0, j)),
    ]
    args = [a, b]
    if res is not None:
        in_specs.append(pl.BlockSpec((tm, tn), lambda i, j: (i, j)))
        args.append(res)
    return pl.pallas_call(
        functools.partial(_mm_kernel, nt=nt, has_res=res is not None),
        grid=(m // tm, n // tn),
        in_specs=in_specs,
        out_specs=pl.BlockSpec((tm, tn), lambda i, j: (i, j)),
        out_shape=jax.ShapeDtypeStruct((m, n), out_dtype),
        compiler_params=_params("parallel", "arbitrary"),
        name="matmul",
    )(*args)


def _merge_kernel(o0_ref, o1_ref, o2_ref, wb_ref, g0_ref, g1_ref, g2_ref, out_ref):
    acc = None
    for o_ref, g_ref, i in ((o0_ref, g0_ref, 0), (o1_ref, g1_ref, 1), (o2_ref, g2_ref, 2)):
        term = jax.nn.sigmoid(g_ref[...]) * jnp.dot(o_ref[...], wb_ref[i], preferred_element_type=F32)
        acc = term if acc is None else acc + term
    out_ref[...] = acc.astype(out_ref.dtype)


def _merge(o_sb, o_rw, o_gdn, wb, gates):
    m, w = o_sb.shape
    d = wb.shape[2]
    tm, tn = _row_tile(m), _col_tile(d)
    nj = d // tn
    o_spec = pl.BlockSpec((tm, w), lambda i, j: (i, 0))

    def g_spec(b):
        return pl.BlockSpec((tm, tn), lambda i, j: (i, j + b * nj))

    return pl.pallas_call(
        _merge_kernel,
        grid=(m // tm, nj),
        in_specs=[o_spec, o_spec, o_spec, pl.BlockSpec((3, w, tn), lambda i, j: (0, 0, j)),
                  g_spec(0), g_spec(1), g_spec(2)],
        out_specs=pl.BlockSpec((tm, tn), lambda i, j: (i, j)),
        out_shape=jax.ShapeDtypeStruct((m, d), BF16),
        compiler_params=_params("parallel", "arbitrary"),
        name="branch_merge",
    )(o_sb, o_rw, o_gdn, wb, gates, gates, gates)


def _peer_hid_kernel(x_ref, u_ref, g_ref, o_ref):
    h = lax.dot_general(x_ref[...], u_ref[...], (((1,), (1,)), ((), ())), preferred_element_type=F32)
    hid = 0.5 * h * (1.0 + lax.erf(h * (2.0 ** -0.5)))
    o_ref[...] = (g_ref[...] * hid).astype(o_ref.dtype)


def _peer_hid(xn, u, g_dense):
    m, d = xn.shape
    e = u.shape[0]
    tm, te = _row_tile(m), _col_tile(e)
    return pl.pallas_call(
        _peer_hid_kernel,
        grid=(m // tm, e // te),
        in_specs=[pl.BlockSpec((tm, d), lambda i, j: (i, 0)), pl.BlockSpec((te, d), lambda i, j: (j, 0)),
                  pl.BlockSpec((tm, te), lambda i, j: (i, j))],
        out_specs=pl.BlockSpec((tm, te), lambda i, j: (i, j)),
        out_shape=jax.ShapeDtypeStruct((m, e), BF16),
        compiler_params=_params("parallel", "arbitrary"),
        name="peer_hidden",
    )(xn, u, g_dense)


def _sb_attn_kernel(q_ref, k_ref, v_ref, o_ref, *, bq, bk, q_off):
    i = pl.program_id(2)
    qpos0 = q_off + i * bq
    q = (q_ref[...] * (SB_DIM ** -0.5)).astype(BF16)
    qpos = qpos0 + lax.broadcasted_iota(jnp.int32, (bq, 1), 0)
    jj = lax.broadcasted_iota(jnp.int32, (bk, bk), 0)
    ss = lax.broadcasted_iota(jnp.int32, (bk, bk), 1)
    tri = (jj >= ss).astype(BF16)
    kb_hi = jnp.maximum(qpos0 + bq - 2, 0) // bk

    def cond(c):
        kb, later, _ = c
        return jnp.logical_and(kb >= 0, jnp.max(later) > SB_LOG_UNDERFLOW)

    def body(c):
        kb, later, acc = c
        off = pl.multiple_of(kb * bk, bk)
        k = k_ref[pl.ds(off, bk), :]
        v = v_ref[pl.ds(off, bk), :]
        z = lax.dot_general(q, k, (((1,), (1,)), ((), ())), preferred_element_type=F32)
        kpos = off + lax.broadcasted_iota(jnp.int32, (1, bk), 1)
        mask = kpos < qpos
        softplus = jnp.maximum(z, 0.0) + jnp.log1p(jnp.exp(-jnp.abs(z)))
        lf = jnp.where(mask, -softplus, 0.0)
        lf_hi = lf.astype(BF16)
        lf_lo = (lf - lf_hi.astype(F32)).astype(BF16)
        rev = (jnp.dot(lf_hi, tri, preferred_element_type=F32)
               + jnp.dot(lf_lo, tri, preferred_element_type=F32))
        att = jnp.where(mask, jnp.exp(z + rev + later), 0.0)
        acc = acc + jnp.dot(att.astype(BF16), v, preferred_element_type=F32)
        return kb - 1, later + rev[:, 0:1], acc

    _, _, acc = lax.while_loop(
        cond, body, (kb_hi, jnp.zeros((bq, 1), F32), jnp.zeros((bq, SB_DIM), F32)))
    o_ref[...] = acc.astype(o_ref.dtype)


def _sb_attention(q, k, v, *, q_off):
    b, tq, w = q.shape
    tk = k.shape[1]
    heads = w // SB_DIM
    bq = min(_row_tile(tq), 128)
    bk = 128
    assert tk % bk == 0 and q_off + tq <= tk + 1
    kv_spec = pl.BlockSpec((None, tk, SB_DIM), lambda bi, h, i: (bi, 0, h))
    q_spec = pl.BlockSpec((None, bq, SB_DIM), lambda bi, h, i: (bi, i, h))
    return pl.pallas_call(
        functools.partial(_sb_attn_kernel, bq=bq, bk=bk, q_off=q_off),
        grid=(b, heads, tq // bq),
        in_specs=[q_spec, kv_spec, kv_spec],
        out_specs=q_spec,
        out_shape=jax.ShapeDtypeStruct((b, tq, w), BF16),
        compiler_params=_params("parallel", "parallel", "arbitrary"),
        name="sb_attention",
    )(q, k, v)


def _rwkv7_chunked(r, logd, k, v, a, b, h0, chunk):
    bsz, t, h, dk = r.shape
    n = t // chunk

    def blocks(x):
        return jnp.moveaxis(x.reshape(bsz, n, chunk, h, x.shape[-1]), (1, 3), (0, 2))

    rc, kc, vc, ac, bc, ld = [blocks(x) for x in (r, k, v, a, b, logd)]
    g_in = jnp.cumsum(ld, axis=3)
    g_ex = g_in - ld
    r_t = rc * jnp.exp(g_in)
    a_t = ac * jnp.exp(g_ex)
    b_h = bc * jnp.exp(-g_in)
    k_h = kc * jnp.exp(-g_in)
    g_last = g_in[..., -1:, :]
    b_end = bc * jnp.exp(g_last - g_in)
    k_end = kc * jnp.exp(g_last - g_in)
    g_tot = jnp.exp(g_last[..., 0, :])
    idx = jnp.arange(chunk)
    strict = idx[:, None] > idx[None, :]
    incl = idx[:, None] >= idx[None, :]

    def pair(x, y, m):
        return jnp.where(m, jnp.einsum('nbhid,nbhjd->nbhij', x, y), 0.0)

    a_ab, a_ak = pair(a_t, b_h, strict), pair(a_t, k_h, strict)
    a_rb, a_rk = pair(r_t, b_h, incl), pair(r_t, k_h, incl)
    rhs = jnp.concatenate([a_t, jnp.einsum('nbhij,nbhje->nbhie', a_ak, vc)], axis=-1)
    sol = lax.linalg.triangular_solve(jnp.eye(chunk, dtype=F32) - a_ab, rhs, left_side=True, lower=True,
                                      unit_diagonal=True)
    w_mat, u0 = sol[..., :dk], sol[..., dk:]
    o_intra = jnp.einsum('nbhij,nbhje->nbhie', a_rk, vc)

    def step(hs, inp):
        w_i, u0_i, r_i, a_rb_i, o_i, b_i, k_i, v_i, gt_i = inp
        u = u0_i + jnp.einsum('bhcd,bhde->bhce', w_i, hs)
        o = o_i + jnp.einsum('bhcd,bhde->bhce', r_i, hs) + jnp.einsum('bhij,bhje->bhie', a_rb_i, u)
        hs = (hs * gt_i[..., None] + jnp.einsum('bhcd,bhce->bhde', b_i, u)
              + jnp.einsum('bhcd,bhce->bhde', k_i, v_i))
        return hs, o

    hs, o = lax.scan(step, h0, (w_mat, u0, r_t, a_rb, o_intra, b_end, k_end, vc, g_tot))
    return jnp.moveaxis(o, (0, 2), (1, 3)).reshape(bsz, t, h, -1), hs


def _rwkv7_time_mix(cols, shift_buf, state, lp, prompt):
    b, t, _ = cols.shape
    rw_w = lp['rw_w0'].shape[0]
    heads, dim = lp['rw_r_k'].shape
    n_dec, n_aaa = lp['rw_w2'].shape[0], lp['rw_a2'].shape[0]
    full = jnp.concatenate([shift_buf.astype(cols.dtype), cols], axis=1)
    prev = full[:, :t]
    xs = (cols + (prev - cols) * lp['rw_mu']).astype(F32)
    r, k, v, xw, xa, xg = jnp.split(xs, [rw_w, 2 * rw_w, 3 * rw_w, 3 * rw_w + n_dec, 3 * rw_w + n_dec + n_aaa],
                                    axis=-1)
    log_w = -jax.nn.softplus(-(lp['rw_w0'] + jnp.tanh(xw) @ lp['rw_w2'])) - 0.5
    log_decay = -jnp.exp(log_w)
    a = jax.nn.sigmoid(lp['rw_a0'] + xa @ lp['rw_a2'])
    gate = jax.nn.sigmoid(xg) @ lp['rw_g2']

    def hd(u):
        return u.reshape(b, t, heads, dim)

    r, k, v, log_decay, a = hd(r), hd(k), hd(v), hd(log_decay), hd(a)
    kk = k * lp['rw_k_k'].reshape(heads, dim)
    kk = kk / jnp.maximum(jnp.sqrt(jnp.sum(kk * kk, axis=-1, keepdims=True)), 1e-12)
    k = k * (1.0 + (a - 1.0) * lp['rw_k_a'].reshape(heads, dim))
    chunk, back = (CHUNK, (-t) % CHUNK) if prompt else (t, 0)

    def pad(u):
        return jnp.pad(u, ((0, 0), (0, back), (0, 0), (0, 0)))

    o, s_new = _rwkv7_chunked(pad(r), pad(log_decay), pad(k), pad(v), pad(-kk), pad(kk * a),
                              state.astype(F32), chunk)
    o = o[:, :t]
    mu = jnp.mean(o, axis=-1, keepdims=True)
    var = jnp.mean(jnp.square(o - mu), axis=-1, keepdims=True)
    gn = ((o - mu) * lax.rsqrt(var + RW_GN_EPS)).reshape(b, t, rw_w) * lp['rw_ln_w'] + lp['rw_ln_b']
    bonus = (jnp.sum(r * k * lp['rw_r_k'], axis=-1, keepdims=True) * v).reshape(b, t, rw_w)
    return (gn + bonus) * gate, full[:, t:], s_new


def _gdn_chunked(q, k, v, beta, g, s0, chunk):
    b, t, h, dk = q.shape
    dv = v.shape[-1]
    n = t // chunk

    def blocks(x):
        return jnp.moveaxis(x.reshape(b, n, chunk, h, x.shape[-1]), (1, 3), (0, 2))

    qc, kc, vc = blocks(q), blocks(k), blocks(v)
    bc = beta.reshape(b, n, chunk, h).transpose(1, 0, 3, 2)
    gc = jnp.cumsum(g.reshape(b, n, chunk, h).transpose(1, 0, 3, 2), axis=-1)
    idx = jnp.arange(chunk)
    incl = idx[:, None] >= idx[None, :]
    strict = idx[:, None] > idx[None, :]
    decay = jnp.exp(jnp.where(incl, gc[..., :, None] - gc[..., None, :], -jnp.inf))
    kb = kc * bc[..., None]
    m = jnp.where(strict, jnp.einsum('nbhid,nbhjd->nbhij', kb, kc) * decay, 0.0)
    lower = m + jnp.eye(chunk, dtype=F32)
    rhs = jnp.concatenate([vc * bc[..., None], kb * jnp.exp(gc)[..., None]], axis=-1)
    sol = lax.linalg.triangular_solve(lower, rhs, left_side=True, lower=True, unit_diagonal=True)
    u, w = sol[..., :dv], sol[..., dv:]
    attn = jnp.einsum('nbhid,nbhjd->nbhij', qc, kc) * decay
    q_dec = qc * jnp.exp(gc)[..., None]
    k_dec = kc * jnp.exp(gc[..., -1:] - gc)[..., None]
    g_last = jnp.exp(gc[..., -1])

    def step(s, inp):
        u_i, w_i, q_i, k_i, attn_i, gl_i = inp
        v_new = u_i - jnp.einsum('bhcd,bhde->bhce', w_i, s)
        o = jnp.einsum('bhcd,bhde->bhce', q_i, s) + jnp.einsum('bhij,bhje->bhie', attn_i, v_new)
        s = s * gl_i[..., None, None] + jnp.einsum('bhcd,bhce->bhde', k_i, v_new)
        return s, o

    s_new, o = lax.scan(step, s0, (u, w, q_dec, k_dec, attn, g_last))
    return jnp.moveaxis(o, (0, 2), (1, 3)).reshape(b, t, h, dv), s_new


def _gated_deltanet(qkv, ab, z, conv_buf, state, lp, prompt, n_meta):
    b, t, _ = qkv.shape
    heads = lp['gdn_a_log'].shape[0]
    dim = lp['gdn_norm_w'].shape[0]
    w = lp['gdn_conv_w']
    width = w.shape[0]
    full = jnp.concatenate([conv_buf.astype(qkv.dtype), qkv], axis=1)
    conv = full[:, 0:t] * w[0]
    for i in range(1, width):
        conv = conv + full[:, i:i + t] * w[i]
    conv_new = full[:, t:]
    qkv_c = jax.nn.silu(conv.astype(F32))
    q, k, v = [u.reshape(b, t, heads, dim) for u in jnp.split(qkv_c, 3, axis=-1)]

    def l2norm(x):
        return x * lax.rsqrt(jnp.sum(x * x, axis=-1, keepdims=True) + 1e-6)

    q = l2norm(q) * (dim ** -0.5)
    k = l2norm(k)
    beta = jax.nn.sigmoid(ab[..., :heads])
    g = -jnp.exp(lp['gdn_a_log'].astype(F32)) * jax.nn.softplus(ab[..., heads:] + lp['gdn_dt_bias'])
    if prompt:
        front = (-n_meta) % CHUNK
        back = (-(front + t)) % CHUNK
        chunk = CHUNK
    else:
        front, back, chunk = 0, 0, t

    def pad(u):
        return jnp.pad(u, ((0, 0), (front, back)) + ((0, 0),) * (u.ndim - 2))

    o, s_new = _gdn_chunked(pad(q), pad(k), pad(v), pad(beta), pad(g), state.astype(F32), chunk)
    o = o[:, front:front + t]
    o = o * lax.rsqrt(jnp.mean(o * o, axis=-1, keepdims=True) + NORM_EPS) * lp['gdn_norm_w']
    o = o.reshape(b, t, heads * dim) * jax.nn.silu(z.astype(F32))
    return o, conv_new, s_new


def _peer_gates(q, subkeys):
    n_tok = q.shape[0]
    heads, _, n_keys, half = subkeys.shape
    qh = q.reshape(n_tok, heads, 2, half)
    s = jnp.einsum('thpc,hpkc->thpk', qh, subkeys.astype(F32))
    sv, _ = lax.top_k(s, PEER_TOPK)
    cand = sv[:, :, 0, :, None] + sv[:, :, 1, None, :]
    fv, _ = lax.top_k(cand.reshape(n_tok, heads, PEER_TOPK * PEER_TOPK), PEER_TOPK)
    top, tau = fv[..., :1], fv[..., -1:]
    inv_z = 1.0 / jnp.sum(jnp.exp(fv - top), axis=-1, keepdims=True)
    full = s[:, :, 0, :, None] + s[:, :, 1, None, :]
    g = jnp.where(full >= tau[..., None], jnp.exp(full - top[..., None]) * inv_z[..., None], 0.0)
    return jnp.sum(g, axis=1).reshape(n_tok, n_keys * n_keys)


def _layer(x, lp, t_valid, sb_cache_k, sb_cache_v, rw_shift, rw_state, gdn_conv, gdn_state, prompt, n_meta):
    b, tp, d = x.shape
    rows = b * tp
    x2 = x.reshape(rows, d)
    xn = _rmsnorm(x2, lp['norm_mix'], BF16)
    cols_sb = _mm(xn, lp['w_sb'])
    cols_rw = _mm(xn, lp['w_rw'])
    cols_gdn = _mm(xn, lp['w_gdn'])
    cols_ab = _mm(xn, lp['w_ab'])
    cols_z = _mm(xn, lp['w_z'])
    gates = _mm(xn, lp['w_gate'])

    sb_w = lp['w_sb'].shape[1] // 3
    q_sb = cols_sb[:, :sb_w].reshape(b, tp, sb_w)
    k_sb = cols_sb[:, sb_w:2 * sb_w].reshape(b, tp, sb_w)
    v_sb = cols_sb[:, 2 * sb_w:].reshape(b, tp, sb_w)
    if prompt:
        o_sb = _sb_attention(q_sb, k_sb.astype(BF16), v_sb.astype(BF16), q_off=0)
    else:
        past = sb_cache_k.shape[1]
        kpad = _round_up(past + tp, 128) - (past + tp)

        def keys(c, n):
            a = jnp.concatenate([c.reshape(b, past, sb_w), n], axis=1)
            return jnp.pad(a, ((0, 0), (0, kpad), (0, 0))).astype(BF16)

        o_sb = _sb_attention(q_sb, keys(sb_cache_k, k_sb), keys(sb_cache_v, v_sb), q_off=past)

    rw_cols = lp['rw_mu'].shape[0]
    gdn_w3 = lp['w_gdn'].shape[1]
    n_ab = 2 * lp['gdn_a_log'].shape[0]
    o_rw, rw_shift_new, rw_state_new = _rwkv7_time_mix(
        cols_rw.reshape(b, tp, -1)[:, :t_valid, :rw_cols], rw_shift, rw_state, lp, prompt)
    o_gdn, gdn_conv_new, gdn_state_new = _gated_deltanet(
        cols_gdn.reshape(b, tp, gdn_w3)[:, :t_valid], cols_ab.reshape(b, tp, -1)[:, :t_valid, :n_ab],
        cols_z.reshape(b, tp, -1)[:, :t_valid], gdn_conv, gdn_state, lp, prompt, n_meta)

    def rows_bf16(o):
        return jnp.pad(o, ((0, 0), (0, tp - t_valid), (0, 0))).reshape(rows, -1).astype(BF16)

    merged = _merge(o_sb.reshape(rows, sb_w), rows_bf16(o_rw), rows_bf16(o_gdn), lp['w_branch'], gates)
    x2 = _mm(merged, lp['w_out'], res=x2)

    xn2 = _rmsnorm(x2, lp['norm_ffn'], BF16)
    q = _mm(xn2, lp['peer_wq'])
    g_dense = _peer_gates(q, lp['peer_subkeys'])
    gh = _peer_hid(xn2, lp['peer_u'], g_dense)
    x2 = _mm(gh, lp['peer_v'], res=x2)

    heads = sb_w // SB_DIM
    new = (k_sb[:, :t_valid].reshape(b, t_valid, heads, SB_DIM), v_sb[:, :t_valid].reshape(b, t_valid, heads, SB_DIM),
           rw_shift_new, rw_state_new, gdn_conv_new, gdn_state_new)
    return x2.reshape(b, tp, d), new


def _pad_cols(w, n):
    return jnp.pad(w, ((0, 0), (0, n - w.shape[1])))


def kernel(x_prompt, x_sample, cache_sb_k, cache_sb_v, state_rw_shift, state_rw, state_gdn_conv, state_gdn, meta_tokens, norm_mix, w_in, rw_mu, rw_w0, rw_w2, rw_a0, rw_a2, rw_g2, rw_k_k, rw_k_a, rw_r_k, rw_ln_w, rw_ln_b, gdn_conv_w, gdn_a_log, gdn_dt_bias, gdn_norm_w, w_branch, w_out, norm_ffn, peer_wq, peer_subkeys, peer_u, peer_v, norm_final):
    depth = w_in.shape[0]
    b_p, seq, d = x_prompt.shape
    n_meta = meta_tokens.shape[0]
    sb_w = w_branch.shape[2]
    rw_cols = rw_mu.shape[1]
    gdn_heads, gdn_dim = gdn_a_log.shape[1], gdn_norm_w.shape[1]
    gdn_w = gdn_heads * gdn_dim
    rw_heads, rw_dim = rw_r_k.shape[1], rw_r_k.shape[2]
    gdn_conv = gdn_conv_w.shape[1]
    off_rw = 3 * sb_w
    off_gdn = off_rw + rw_cols
    off_ab = off_gdn + 3 * gdn_w
    off_z = off_ab + 2 * gdn_heads
    off_gate = off_z + gdn_w

    t_p = n_meta + seq
    tp_pad = _round_up(t_p, 256)
    hp = jnp.concatenate([jnp.broadcast_to(meta_tokens.astype(x_prompt.dtype)[None], (b_p, n_meta, d)), x_prompt],
                         axis=1)
    hp = jnp.pad(hp, ((0, 0), (0, tp_pad - t_p), (0, 0)))
    hs = x_sample
    t_s = x_sample.shape[1]

    p_rows, s_rows = [], []
    for l in range(depth):
        wl = w_in[l]
        lp = dict(
            norm_mix=norm_mix[l], norm_ffn=norm_ffn[l],
            w_sb=wl[:, :off_rw].astype(BF16),
            w_rw=_pad_cols(wl[:, off_rw:off_gdn], _round_up(rw_cols, LANE)).astype(BF16),
            w_gdn=wl[:, off_gdn:off_ab].astype(BF16),
            w_ab=_pad_cols(wl[:, off_ab:off_z], LANE).astype(BF16),
            w_z=wl[:, off_z:off_gate].astype(BF16),
            w_gate=wl[:, off_gate:].astype(BF16),
            rw_mu=rw_mu[l], rw_w0=rw_w0[l], rw_w2=rw_w2[l], rw_a0=rw_a0[l], rw_a2=rw_a2[l], rw_g2=rw_g2[l],
            rw_k_k=rw_k_k[l], rw_k_a=rw_k_a[l], rw_r_k=rw_r_k[l], rw_ln_w=rw_ln_w[l], rw_ln_b=rw_ln_b[l],
            gdn_conv_w=gdn_conv_w[l], gdn_a_log=gdn_a_log[l], gdn_dt_bias=gdn_dt_bias[l], gdn_norm_w=gdn_norm_w[l],
            w_branch=w_branch[l].astype(BF16), w_out=w_out[l].astype(BF16),
            peer_wq=peer_wq[l].astype(BF16), peer_subkeys=peer_subkeys[l],
            peer_u=peer_u[l].astype(BF16), peer_v=peer_v[l].astype(BF16))
        hp, p_new = _layer(
            hp, lp, t_p, None, None,
            jnp.zeros((b_p, 1, rw_cols), F32), jnp.zeros((b_p, rw_heads, rw_dim, rw_dim), F32),
            jnp.zeros((b_p, gdn_conv - 1, 3 * gdn_w), F32), jnp.zeros((b_p, gdn_heads, gdn_dim, gdn_dim), F32),
            True, n_meta)
        hs, s_new = _layer(
            hs, lp, t_s, cache_sb_k[l], cache_sb_v[l], state_rw_shift[l], state_rw[l],
            state_gdn_conv[l], state_gdn[l], False, n_meta)
        p_rows.append(p_new)
        s_rows.append(s_new)

    p_out = [jnp.stack(z) for z in zip(*p_rows)]
    s_out = [jnp.stack(z) for z in zip(*s_rows)]
    y_prompt = _rmsnorm(hp.reshape(b_p * tp_pad, d), norm_final, F32).reshape(b_p, tp_pad, d)[:, n_meta:t_p]
    b_s = hs.shape[0]
    y_sample = _rmsnorm(hs.reshape(b_s * t_s, d), norm_final, F32).reshape(b_s, t_s, d)
    return (y_prompt, y_sample, *p_out, *s_out)
```

```python
import functools

import jax
import jax.numpy as jnp
from jax import lax
from jax.experimental import pallas as pl
from jax.experimental.pallas import tpu as pltpu

F32 = jnp.float32
BF16 = jnp.bfloat16

V7X_VMEM_LIMIT_BYTES = 56 * 2**20
LANE = 128

NORM_EPS = 1e-6
RW_GN_EPS = 64e-5
CHUNK = 64
SB_DIM = 128
PEER_TOPK = 16
SB_LOG_UNDERFLOW = -104.0


def _round_up(n, m):
    return -(-n // m) * m


def _params(*sem):
    return pltpu.CompilerParams(dimension_semantics=sem, vmem_limit_bytes=V7X_VMEM_LIMIT_BYTES)


def _row_tile(rows):
    for t in (1280, 640, 256, 128, 64, 32, 16, 8):
        if rows % t == 0:
            return t
    raise ValueError(f"row count {rows} is not a multiple of 8")


def _col_tile(n):
    for t in (512, 384, 256, 128):
        if n % t == 0:
            return t
    raise ValueError(f"column count {n} is not a multiple of {LANE}")


def _rmsnorm_kernel(x_ref, g_ref, o_ref):
    x = x_ref[...]
    ms = jnp.mean(x * x, axis=-1, keepdims=True)
    o_ref[...] = (x * lax.rsqrt(ms + NORM_EPS) * g_ref[...]).astype(o_ref.dtype)


def _rmsnorm(x, g, out_dtype):
    rows, d = x.shape
    tr = min(_row_tile(rows), 256)
    return pl.pallas_call(
        _rmsnorm_kernel,
        grid=(rows // tr,),
        in_specs=[pl.BlockSpec((tr, d), lambda i: (i, 0)), pl.BlockSpec((1, d), lambda i: (0, 0))],
        out_specs=pl.BlockSpec((tr, d), lambda i: (i, 0)),
        out_shape=jax.ShapeDtypeStruct((rows, d), out_dtype),
        compiler_params=_params("parallel"),
        name="rmsnorm",
    )(x, g.reshape(1, d).astype(F32))


def _mm_kernel(a_ref, b_ref, *rest, nt, has_res):
    o_ref = rest[-1]
    dn = (((1,), (1,) if nt else (0,)), ((), ()))
    acc = lax.dot_general(a_ref[...], b_ref[...], dn, preferred_element_type=F32)
    if has_res:
        acc = acc + rest[0][...]
    o_ref[...] = acc.astype(o_ref.dtype)


def _mm(a, b, *, nt=False, res=None, out_dtype=F32):
    m, k = a.shape
    n = b.shape[0] if nt else b.shape[1]
    tm, tn = _row_tile(m), _col_tile(n)
    in_specs = [
        pl.BlockSpec((tm, k), lambda i, j: (i, 0)),
        pl.BlockSpec((tn, k), lambda i, j: (j, 0)) if nt else pl.BlockSpec((k, tn), lambda i, j: (0, j)),
    ]
    args = [a, b]
    if res is not None:
        in_specs.append(pl.BlockSpec((tm, tn), lambda i, j: (i, j)))
        args.append(res)
    return pl.pallas_call(
        functools.partial(_mm_kernel, nt=nt, has_res=res is not None),
        grid=(m // tm, n // tn),
        in_specs=in_specs,
        out_specs=pl.BlockSpec((tm, tn), lambda i, j: (i, j)),
        out_shape=jax.ShapeDtypeStruct((m, n), out_dtype),
        compiler_params=_params("parallel", "arbitrary"),
        name="matmul",
    )(*args)


def _merge_kernel(o0_ref, o1_ref, o2_ref, wb_ref, g0_ref, g1_ref, g2_ref, out_ref):
    acc = None
    for o_ref, g_ref, i in ((o0_ref, g0_ref, 0), (o1_ref, g1_ref, 1), (o2_ref, g2_ref, 2)):
        term = jax.nn.sigmoid(g_ref[...]) * jnp.dot(o_ref[...], wb_ref[i], preferred_element_type=F32)
        acc = term if acc is None else acc + term
    out_ref[...] = acc.astype(out_ref.dtype)


def _merge(o_sb, o_rw, o_gdn, wb, gates):
    m, w = o_sb.shape
    d = wb.shape[2]
    tm, tn = _row_tile(m), _col_tile(d)
    nj = d // tn
    o_spec = pl.BlockSpec((tm, w), lambda i, j: (i, 0))

    def g_spec(b):
        return pl.BlockSpec((tm, tn), lambda i, j: (i, j + b * nj))

    return pl.pallas_call(
        _merge_kernel,
        grid=(m // tm, nj),
        in_specs=[o_spec, o_spec, o_spec, pl.BlockSpec((3, w, tn), lambda i, j: (0, 0, j)),
                  g_spec(0), g_spec(1), g_spec(2)],
        out_specs=pl.BlockSpec((tm, tn), lambda i, j: (i, j)),
        out_shape=jax.ShapeDtypeStruct((m, d), BF16),
        compiler_params=_params("parallel", "arbitrary"),
        name="branch_merge",
    )(o_sb, o_rw, o_gdn, wb, gates, gates, gates)


def _peer_hid_kernel(x_ref, u_ref, g_ref, o_ref):
    h = lax.dot_general(x_ref[...], u_ref[...], (((1,), (1,)), ((), ())), preferred_element_type=F32)
    hid = 0.5 * h * (1.0 + lax.erf(h * (2.0 ** -0.5)))
    o_ref[...] = (g_ref[...] * hid).astype(o_ref.dtype)


def _peer_hid(xn, u, g_dense):
    m, d = xn.shape
    e = u.shape[0]
    tm, te = _row_tile(m), _col_tile(e)
    return pl.pallas_call(
        _peer_hid_kernel,
        grid=(m // tm, e // te),
        in_specs=[pl.BlockSpec((tm, d), lambda i, j: (i, 0)), pl.BlockSpec((te, d), lambda i, j: (j, 0)),
                  pl.BlockSpec((tm, te), lambda i, j: (i, j))],
        out_specs=pl.BlockSpec((tm, te), lambda i, j: (i, j)),
        out_shape=jax.ShapeDtypeStruct((m, e), BF16),
        compiler_params=_params("parallel", "arbitrary"),
        name="peer_hidden",
    )(xn, u, g_dense)


def _sb_attn_kernel(q_ref, k_ref, v_ref, o_ref, *, bq, bk, q_off):
    i = pl.program_id(2)
    qpos0 = q_off + i * bq
    q = (q_ref[...] * (SB_DIM ** -0.5)).astype(BF16)
    qpos = qpos0 + lax.broadcasted_iota(jnp.int32, (bq, 1), 0)
    jj = lax.broadcasted_iota(jnp.int32, (bk, bk), 0)
    ss = lax.broadcasted_iota(jnp.int32, (bk, bk), 1)
    tri = (jj >= ss).astype(BF16)
    kb_hi = jnp.maximum(qpos0 + bq - 2, 0) // bk

    def cond(c):
        kb, later, _ = c
        return jnp.logical_and(kb >= 0, jnp.max(later) > SB_LOG_UNDERFLOW)

    def body(c):
        kb, later, acc = c
        off = pl.multiple_of(kb * bk, bk)
        k = k_ref[pl.ds(off, bk), :]
        v = v_ref[pl.ds(off, bk), :]
        z = lax.dot_general(q, k, (((1,), (1,)), ((), ())), preferred_element_type=F32)
        kpos = off + lax.broadcasted_iota(jnp.int32, (1, bk), 1)
        mask = kpos < qpos
        softplus = jnp.maximum(z, 0.0) + jnp.log1p(jnp.exp(-jnp.abs(z)))
        lf = jnp.where(mask, -softplus, 0.0)
        lf_hi = lf.astype(BF16)
        lf_lo = (lf - lf_hi.astype(F32)).astype(BF16)
        rev = (jnp.dot(lf_hi, tri, preferred_element_type=F32)
               + jnp.dot(lf_lo, tri, preferred_element_type=F32))
        att = jnp.where(mask, jnp.exp(z + rev + later), 0.0)
        acc = acc + jnp.dot(att.astype(BF16), v, preferred_element_type=F32)
        return kb - 1, later + rev[:, 0:1], acc

    _, _, acc = lax.while_loop(
        cond, body, (kb_hi, jnp.zeros((bq, 1), F32), jnp.zeros((bq, SB_DIM), F32)))
    o_ref[...] = acc.astype(o_ref.dtype)


def _sb_attention(q, k, v, *, q_off):
    b, tq, w = q.shape
    tk = k.shape[1]
    heads = w // SB_DIM
    bq = min(_row_tile(tq), 128)
    bk = 128
    assert tk % bk == 0 and q_off + tq <= tk + 1
    kv_spec = pl.BlockSpec((None, tk, SB_DIM), lambda bi, h, i: (bi, 0, h))
    q_spec = pl.BlockSpec((None, bq, SB_DIM), lambda bi, h, i: (bi, i, h))
    return pl.pallas_call(
        functools.partial(_sb_attn_kernel, bq=bq, bk=bk, q_off=q_off),
        grid=(b, heads, tq // bq),
        in_specs=[q_spec, kv_spec, kv_spec],
        out_specs=q_spec,
        out_shape=jax.ShapeDtypeStruct((b, tq, w), BF16),
        compiler_params=_params("parallel", "parallel", "arbitrary"),
        name="sb_attention",
    )(q, k, v)


def _rwkv7_chunked(r, logd, k, v, a, b, h0, chunk):
    bsz, t, h, dk = r.shape
    n = t // chunk

    def blocks(x):
        return jnp.moveaxis(x.reshape(bsz, n, chunk, h, x.shape[-1]), (1, 3), (0, 2))

    rc, kc, vc, ac, bc, ld = [blocks(x) for x in (r, k, v, a, b, logd)]
    g_in = jnp.cumsum(ld, axis=3)
    g_ex = g_in - ld
    r_t = rc * jnp.exp(g_in)
    a_t = ac * jnp.exp(g_ex)
    b_h = bc * jnp.exp(-g_in)
    k_h = kc * jnp.exp(-g_in)
    g_last = g_in[..., -1:, :]
    b_end = bc * jnp.exp(g_last - g_in)
    k_end = kc * jnp.exp(g_last - g_in)
    g_tot = jnp.exp(g_last[..., 0, :])
    idx = jnp.arange(chunk)
    strict = idx[:, None] > idx[None, :]
    incl = idx[:, None] >= idx[None, :]

    def pair(x, y, m):
        return jnp.where(m, jnp.einsum('nbhid,nbhjd->nbhij', x, y), 0.0)

    a_ab, a_ak = pair(a_t, b_h, strict), pair(a_t, k_h, strict)
    a_rb, a_rk = pair(r_t, b_h, incl), pair(r_t, k_h, incl)
    rhs = jnp.concatenate([a_t, jnp.einsum('nbhij,nbhje->nbhie', a_ak, vc)], axis=-1)
    sol = lax.linalg.triangular_solve(jnp.eye(chunk, dtype=F32) - a_ab, rhs, left_side=True, lower=True,
                                      unit_diagonal=True)
    w_mat, u0 = sol[..., :dk], sol[..., dk:]
    o_intra = jnp.einsum('nbhij,nbhje->nbhie', a_rk, vc)

    def step(hs, inp):
        w_i, u0_i, r_i, a_rb_i, o_i, b_i, k_i, v_i, gt_i = inp
        u = u0_i + jnp.einsum('bhcd,bhde->bhce', w_i, hs)
        o = o_i + jnp.einsum('bhcd,bhde->bhce', r_i, hs) + jnp.einsum('bhij,bhje->bhie', a_rb_i, u)
        hs = (hs * gt_i[..., None] + jnp.einsum('bhcd,bhce->bhde', b_i, u)
              + jnp.einsum('bhcd,bhce->bhde', k_i, v_i))
        return hs, o

    hs, o = lax.scan(step, h0, (w_mat, u0, r_t, a_rb, o_intra, b_end, k_end, vc, g_tot))
    return jnp.moveaxis(o, (0, 2), (1, 3)).reshape(bsz, t, h, -1), hs


def _rwkv7_time_mix(cols, shift_buf, state, lp, prompt):
    b, t, _ = cols.shape
    rw_w = lp['rw_w0'].shape[0]
    heads, dim = lp['rw_r_k'].shape
    n_dec, n_aaa = lp['rw_w2'].shape[0], lp['rw_a2'].shape[0]
    full = jnp.concatenate([shift_buf.astype(cols.dtype), cols], axis=1)
    prev = full[:, :t]
    xs = (cols + (prev - cols) * lp['rw_mu']).astype(F32)
    r, k, v, xw, xa, xg = jnp.split(xs, [rw_w, 2 * rw_w, 3 * rw_w, 3 * rw_w + n_dec, 3 * rw_w + n_dec + n_aaa],
                                    axis=-1)
    log_w = -jax.nn.softplus(-(lp['rw_w0'] + jnp.tanh(xw) @ lp['rw_w2'])) - 0.5
    log_decay = -jnp.exp(log_w)
    a = jax.nn.sigmoid(lp['rw_a0'] + xa @ lp['rw_a2'])
    gate = jax.nn.sigmoid(xg) @ lp['rw_g2']

    def hd(u):
        return u.reshape(b, t, heads, dim)

    r, k, v, log_decay, a = hd(r), hd(k), hd(v), hd(log_decay), hd(a)
    kk = k * lp['rw_k_k'].reshape(heads, dim)
    kk = kk / jnp.maximum(jnp.sqrt(jnp.sum(kk * kk, axis=-1, keepdims=True)), 1e-12)
    k = k * (1.0 + (a - 1.0) * lp['rw_k_a'].reshape(heads, dim))
    chunk, back = (CHUNK, (-t) % CHUNK) if prompt else (t, 0)

    def pad(u):
        return jnp.pad(u, ((0, 0), (0, back), (0, 0), (0, 0)))

    o, s_new = _rwkv7_chunked(pad(r), pad(log_decay), pad(k), pad(v), pad(-kk), pad(kk * a),
                              state.astype(F32), chunk)
    o = o[:, :t]
    mu = jnp.mean(o, axis=-1, keepdims=True)
    var = jnp.mean(jnp.square(o - mu), axis=-1, keepdims=True)
    gn = ((o - mu) * lax.rsqrt(var + RW_GN_EPS)).reshape(b, t, rw_w) * lp['rw_ln_w'] + lp['rw_ln_b']
    bonus = (jnp.sum(r * k * lp['rw_r_k'], axis=-1, keepdims=True) * v).reshape(b, t, rw_w)
    return (gn + bonus) * gate, full[:, t:], s_new


def _gdn_chunked(q, k, v, beta, g, s0, chunk):
    b, t, h, dk = q.shape
    dv = v.shape[-1]
    n = t // chunk

    def blocks(x):
        return jnp.moveaxis(x.reshape(b, n, chunk, h, x.shape[-1]), (1, 3), (0, 2))

    qc, kc, vc = blocks(q), blocks(k), blocks(v)
    bc = beta.reshape(b, n, chunk, h).transpose(1, 0, 3, 2)
    gc = jnp.cumsum(g.reshape(b, n, chunk, h).transpose(1, 0, 3, 2), axis=-1)
    idx = jnp.arange(chunk)
    incl = idx[:, None] >= idx[None, :]
    strict = idx[:, None] > idx[None, :]
    decay = jnp.exp(jnp.where(incl, gc[..., :, None] - gc[..., None, :], -jnp.inf))
    kb = kc * bc[..., None]
    m = jnp.where(strict, jnp.einsum('nbhid,nbhjd->nbhij', kb, kc) * decay, 0.0)
    lower = m + jnp.eye(chunk, dtype=F32)
    rhs = jnp.concatenate([vc * bc[..., None], kb * jnp.exp(gc)[..., None]], axis=-1)
    sol = lax.linalg.triangular_solve(lower, rhs, left_side=True, lower=True, unit_diagonal=True)
    u, w = sol[..., :dv], sol[..., dv:]
    attn = jnp.einsum('nbhid,nbhjd->nbhij', qc, kc) * decay
    q_dec = qc * jnp.exp(gc)[..., None]
    k_dec = kc * jnp.exp(gc[..., -1:] - gc)[..., None]
    g_last = jnp.exp(gc[..., -1])

    def step(s, inp):
        u_i, w_i, q_i, k_i, attn_i, gl_i = inp
        v_new = u_i - jnp.einsum('bhcd,bhde->bhce', w_i, s)
        o = jnp.einsum('bhcd,bhde->bhce', q_i, s) + jnp.einsum('bhij,bhje->bhie', attn_i, v_new)
        s = s * gl_i[..., None, None] + jnp.einsum('bhcd,bhce->bhde', k_i, v_new)
        return s, o

    s_new, o = lax.scan(step, s0, (u, w, q_dec, k_dec, attn, g_last))
    return jnp.moveaxis(o, (0, 2), (1, 3)).reshape(b, t, h, dv), s_new


def _gated_deltanet(qkv, ab, z, conv_buf, state, lp, prompt, n_meta):
    b, t, _ = qkv.shape
    heads = lp['gdn_a_log'].shape[0]
    dim = lp['gdn_norm_w'].shape[0]
    w = lp['gdn_conv_w']
    width = w.shape[0]
    full = jnp.concatenate([conv_buf.astype(qkv.dtype), qkv], axis=1)
    conv = full[:, 0:t] * w[0]
    for i in range(1, width):
        conv = conv + full[:, i:i + t] * w[i]
    conv_new = full[:, t:]
    qkv_c = jax.nn.silu(conv.astype(F32))
    q, k, v = [u.reshape(b, t, heads, dim) for u in jnp.split(qkv_c, 3, axis=-1)]

    def l2norm(x):
        return x * lax.rsqrt(jnp.sum(x * x, axis=-1, keepdims=True) + 1e-6)

    q = l2norm(q) * (dim ** -0.5)
    k = l2norm(k)
    beta = jax.nn.sigmoid(ab[..., :heads])
    g = -jnp.exp(lp['gdn_a_log'].astype(F32)) * jax.nn.softplus(ab[..., heads:] + lp['gdn_dt_bias'])
    if prompt:
        front = (-n_meta) % CHUNK
        back = (-(front + t)) % CHUNK
        chunk = CHUNK
    else:
        front, back, chunk = 0, 0, t

    def pad(u):
        return jnp.pad(u, ((0, 0), (front, back)) + ((0, 0),) * (u.ndim - 2))

    o, s_new = _gdn_chunked(pad(q), pad(k), pad(v), pad(beta), pad(g), state.astype(F32), chunk)
    o = o[:, front:front + t]
    o = o * lax.rsqrt(jnp.mean(o * o, axis=-1, keepdims=True) + NORM_EPS) * lp['gdn_norm_w']
    o = o.reshape(b, t, heads * dim) * jax.nn.silu(z.astype(F32))
    return o, conv_new, s_new


def _split_bf16(x):
    hi = x.astype(BF16)
    return hi, (x - hi.astype(F32)).astype(BF16)


def _peer_gate_kernel(qt_ref, khi_ref, klo_ref, g_ref, s_ref, sv_ref, cand_ref, fv_ref, e0_ref, e1_ref, gt_ref, *,
                      heads, n_keys, half):
    tt = qt_ref.shape[1]
    neg_inf = jnp.float32(-jnp.inf)
    key_iota = lax.broadcasted_iota(jnp.int32, (n_keys, tt), 0)

    for grp in range(2 * heads):
        p, h = divmod(grp, heads)
        q_hi, q_lo = _split_bf16(qt_ref[(2 * h + p) * half:(2 * h + p + 1) * half, :])
        s = (jnp.dot(khi_ref[grp], q_hi, preferred_element_type=F32)
             + jnp.dot(khi_ref[grp], q_lo, preferred_element_type=F32)
             + jnp.dot(klo_ref[grp], q_hi, preferred_element_type=F32))
        s_ref[grp * n_keys:(grp + 1) * n_keys, :] = s

        def extract(n, blk, grp=grp):
            m = jnp.max(blk, axis=0, keepdims=True)
            first = jnp.min(jnp.where(blk == m, key_iota, n_keys), axis=0, keepdims=True)
            sv_ref[n, grp:grp + 1, :] = m
            return jnp.where(key_iota == first, neg_inf, blk)

        lax.fori_loop(0, PEER_TOPK, extract, s)

    pairs = [(i, j) for i in range(PEER_TOPK) for j in range(PEER_TOPK) if (i + 1) * (j + 1) <= PEER_TOPK]
    for pos, (i, j) in enumerate(pairs):
        cand_ref[pos] = sv_ref[i, 0:heads, :] + sv_ref[j, heads:2 * heads, :]
    pair_iota = lax.broadcasted_iota(jnp.int32, (len(pairs), heads, tt), 0)

    def extract_pair(n, carry):
        c = cand_ref[...]
        m = jnp.max(c, axis=0)
        first = jnp.min(jnp.where(c == m[None], pair_iota, len(pairs)), axis=0)
        fv_ref[n] = m
        cand_ref[...] = jnp.where(pair_iota == first[None], neg_inf, c)
        return carry

    lax.fori_loop(0, PEER_TOPK, extract_pair, 0)
    fv = fv_ref[...]
    top, tau = fv[0], fv[PEER_TOPK - 1]
    inv_z = 1.0 / jnp.sum(jnp.exp(fv - top[None]), axis=0)

    m0, m1 = sv_ref[0, 0:heads, :], sv_ref[0, heads:2 * heads, :]
    for h in range(heads):
        s0 = s_ref[h * n_keys:(h + 1) * n_keys, :]
        s1 = s_ref[(heads + h) * n_keys:(heads + h + 1) * n_keys, :]
        e0_ref[h * n_keys:(h + 1) * n_keys, :] = jnp.exp(s0 - m0[h:h + 1, :]) * inv_z[h:h + 1, :]
        e1_ref[h * n_keys:(h + 1) * n_keys, :] = jnp.exp(s1 - m1[h:h + 1, :])

    def expert_rows(a, carry):
        acc = jnp.zeros((n_keys, tt), F32)
        for h in range(heads):
            s_ab = s_ref[pl.ds(h * n_keys + a, 1), :] + s_ref[(heads + h) * n_keys:(heads + h + 1) * n_keys, :]
            w_ab = e0_ref[pl.ds(h * n_keys + a, 1), :] * e1_ref[h * n_keys:(h + 1) * n_keys, :]
            acc = acc + jnp.where(s_ab >= tau[h:h + 1, :], w_ab, 0.0)
        gt_ref[pl.ds(pl.multiple_of(a * n_keys, n_keys), n_keys), :] = acc
        return carry

    lax.fori_loop(0, n_keys, expert_rows, 0)
    g_ref[...] = gt_ref[...].T


def _peer_gates(qt, subkeys):
    n_tok = qt.shape[1]
    heads, _, n_keys, half = subkeys.shape
    khi, klo = _split_bf16(jnp.transpose(subkeys.astype(F32), (1, 0, 2, 3)).reshape(2 * heads, n_keys, half))
    tt = 256 if n_tok % 256 == 0 else n_tok
    n_exp = n_keys * n_keys
    n_pairs = sum(PEER_TOPK // (i + 1) for i in range(PEER_TOPK))
    k_spec =pl.BlockSpec((2 * heads, n_keys, half), lambda i: (0, 0, 0))
    return pl.pallas_call(
        functools.partial(_peer_gate_kernel, heads=heads, n_keys=n_keys, half=half),
        grid=(n_tok // tt,),
        in_specs=[pl.BlockSpec((2 * heads * half, tt), lambda i: (0, i)), k_spec, k_spec],
        out_specs=pl.BlockSpec((tt, n_exp), lambda i: (i, 0)),
        out_shape=jax.ShapeDtypeStruct((n_tok, n_exp), F32),
        scratch_shapes=[pltpu.VMEM((2 * heads * n_keys, tt), F32), pltpu.VMEM((PEER_TOPK, 2 * heads, tt), F32),
                        pltpu.VMEM((n_pairs, heads, tt), F32), pltpu.VMEM((PEER_TOPK, heads, tt), F32),
                        pltpu.VMEM((heads * n_keys, tt), F32), pltpu.VMEM((heads * n_keys, tt), F32),
                        pltpu.VMEM((n_exp, tt), F32)],
        compiler_params=_params("parallel"),
        name="peer_gates",
    )(qt, khi, klo)


def _layer(x, lp, t_valid, sb_cache_k, sb_cache_v, rw_shift, rw_state, gdn_conv, gdn_state, prompt, n_meta):
    b, tp, d = x.shape
    rows = b * tp
    x2 = x.reshape(rows, d)
    xn = _rmsnorm(x2, lp['norm_mix'], BF16)
    cols_sb = _mm(xn, lp['w_sb'])
    cols_rw = _mm(xn, lp['w_rw'])
    cols_gdn = _mm(xn, lp['w_gdn'])
    cols_ab = _mm(xn, lp['w_ab'])
    cols_z = _mm(xn, lp['w_z'])
    gates = _mm(xn, lp['w_gate'])

    sb_w = lp['w_sb'].shape[1] // 3
    q_sb = cols_sb[:, :sb_w].reshape(b, tp, sb_w)
    k_sb = cols_sb[:, sb_w:2 * sb_w].reshape(b, tp, sb_w)
    v_sb = cols_sb[:, 2 * sb_w:].reshape(b, tp, sb_w)
    if prompt:
        o_sb = _sb_attention(q_sb, k_sb.astype(BF16), v_sb.astype(BF16), q_off=0)
    else:
        past = sb_cache_k.shape[1]
        kpad = _round_up(past + tp, 128) - (past + tp)

        def keys(c, n):
            a = jnp.concatenate([c.reshape(b, past, sb_w), n], axis=1)
            return jnp.pad(a, ((0, 0), (0, kpad), (0, 0))).astype(BF16)

        o_sb = _sb_attention(q_sb, keys(sb_cache_k, k_sb), keys(sb_cache_v, v_sb), q_off=past)

    rw_cols = lp['rw_mu'].shape[0]
    gdn_w3 = lp['w_gdn'].shape[1]
    n_ab = 2 * lp['gdn_a_log'].shape[0]
    o_rw, rw_shift_new, rw_state_new = _rwkv7_time_mix(
        cols_rw.reshape(b, tp, -1)[:, :t_valid, :rw_cols], rw_shift, rw_state, lp, prompt)
    o_gdn, gdn_conv_new, gdn_state_new = _gated_deltanet(
        cols_gdn.reshape(b, tp, gdn_w3)[:, :t_valid], cols_ab.reshape(b, tp, -1)[:, :t_valid, :n_ab],
        cols_z.reshape(b, tp, -1)[:, :t_valid], gdn_conv, gdn_state, lp, prompt, n_meta)

    def rows_bf16(o):
        return jnp.pad(o, ((0, 0), (0, tp - t_valid), (0, 0))).reshape(rows, -1).astype(BF16)

    merged = _merge(o_sb.reshape(rows, sb_w), rows_bf16(o_rw), rows_bf16(o_gdn), lp['w_branch'], gates)
    x2 = _mm(merged, lp['w_out'], res=x2)

    xn2 = _rmsnorm(x2, lp['norm_ffn'], BF16)
    qt = _mm(lp['peer_wq_t'], xn2, nt=True)
    g_dense = _peer_gates(qt, lp['peer_subkeys'])
    gh = _peer_hid(xn2, lp['peer_u'], g_dense)
    x2 = _mm(gh, lp['peer_v'], res=x2)

    heads = sb_w // SB_DIM
    new = (k_sb[:, :t_valid].reshape(b, t_valid, heads, SB_DIM), v_sb[:, :t_valid].reshape(b, t_valid, heads, SB_DIM),
           rw_shift_new, rw_state_new, gdn_conv_new, gdn_state_new)
    return x2.reshape(b, tp, d), new


def _pad_cols(w, n):
    return jnp.pad(w, ((0, 0), (0, n - w.shape[1])))


def kernel(x_prompt, x_sample, cache_sb_k, cache_sb_v, state_rw_shift, state_rw, state_gdn_conv, state_gdn, meta_tokens, norm_mix, w_in, rw_mu, rw_w0, rw_w2, rw_a0, rw_a2, rw_g2, rw_k_k, rw_k_a, rw_r_k, rw_ln_w, rw_ln_b, gdn_conv_w, gdn_a_log, gdn_dt_bias, gdn_norm_w, w_branch, w_out, norm_ffn, peer_wq, peer_subkeys, peer_u, peer_v, norm_final):
    depth = w_in.shape[0]
    b_p, seq, d = x_prompt.shape
    n_meta = meta_tokens.shape[0]
    sb_w = w_branch.shape[2]
    rw_cols = rw_mu.shape[1]
    gdn_heads, gdn_dim = gdn_a_log.shape[1], gdn_norm_w.shape[1]
    gdn_w = gdn_heads * gdn_dim
    rw_heads, rw_dim = rw_r_k.shape[1], rw_r_k.shape[2]
    gdn_conv = gdn_conv_w.shape[1]
    off_rw = 3 * sb_w
    off_gdn = off_rw + rw_cols
    off_ab = off_gdn + 3 * gdn_w
    off_z = off_ab + 2 * gdn_heads
    off_gate = off_z + gdn_w

    t_p = n_meta + seq
    tp_pad = _round_up(t_p, 256)
    hp = jnp.concatenate([jnp.broadcast_to(meta_tokens.astype(x_prompt.dtype)[None], (b_p, n_meta, d)), x_prompt],
                         axis=1)
    hp = jnp.pad(hp, ((0, 0), (0, tp_pad - t_p), (0, 0)))
    hs = x_sample
    t_s = x_sample.shape[1]

    p_rows, s_rows = [], []
    for l in range(depth):
        wl = w_in[l]
        lp = dict(
            norm_mix=norm_mix[l], norm_ffn=norm_ffn[l],
            w_sb=wl[:, :off_rw].astype(BF16),
            w_rw=_pad_cols(wl[:, off_rw:off_gdn], _round_up(rw_cols, LANE)).astype(BF16),
            w_gdn=wl[:, off_gdn:off_ab].astype(BF16),
            w_ab=_pad_cols(wl[:, off_ab:off_z], LANE).astype(BF16),
            w_z=wl[:, off_z:off_gate].astype(BF16),
            w_gate=wl[:, off_gate:].astype(BF16),
            rw_mu=rw_mu[l], rw_w0=rw_w0[l], rw_w2=rw_w2[l], rw_a0=rw_a0[l], rw_a2=rw_a2[l], rw_g2=rw_g2[l],
            rw_k_k=rw_k_k[l], rw_k_a=rw_k_a[l], rw_r_k=rw_r_k[l], rw_ln_w=rw_ln_w[l], rw_ln_b=rw_ln_b[l],
            gdn_conv_w=gdn_conv_w[l], gdn_a_log=gdn_a_log[l], gdn_dt_bias=gdn_dt_bias[l], gdn_norm_w=gdn_norm_w[l],
            w_branch=w_branch[l].astype(BF16), w_out=w_out[l].astype(BF16),
            peer_wq_t=peer_wq[l].T.astype(BF16), peer_subkeys=peer_subkeys[l],
            peer_u=peer_u[l].astype(BF16), peer_v=peer_v[l].astype(BF16))
        hp, p_new = _layer(
            hp, lp, t_p, None, None,
            jnp.zeros((b_p, 1, rw_cols), F32), jnp.zeros((b_p, rw_heads, rw_dim, rw_dim), F32),
            jnp.zeros((b_p, gdn_conv - 1, 3 * gdn_w), F32), jnp.zeros((b_p, gdn_heads, gdn_dim, gdn_dim), F32),
            True, n_meta)
        hs, s_new = _layer(
            hs, lp, t_s, cache_sb_k[l], cache_sb_v[l], state_rw_shift[l], state_rw[l],
            state_gdn_conv[l], state_gdn[l], False, n_meta)
        p_rows.append(p_new)
        s_rows.append(s_new)

    p_out = [jnp.stack(z) for z in zip(*p_rows)]
    s_out = [jnp.stack(z) for z in zip(*s_rows)]
    y_prompt = _rmsnorm(hp.reshape(b_p * tp_pad, d), norm_final, F32).reshape(b_p, tp_pad, d)[:, n_meta:t_p]
    b_s = hs.shape[0]
    y_sample = _rmsnorm(hs.reshape(b_s * t_s, d), norm_final, F32).reshape(b_s, t_s, d)
    return (y_prompt, y_sample, *p_out, *s_out)
```

```python
import functools

import jax
import jax.numpy as jnp
from jax import lax
from jax.experimental import pallas as pl
from jax.experimental.pallas import tpu as pltpu

F32 = jnp.float32
BF16 = jnp.bfloat16

V7X_VMEM_LIMIT_BYTES = 56 * 2**20
LANE = 128
SUBLANE = 8

NORM_EPS = 1e-6
RW_GN_EPS = 64e-5
CHUNK = 64
SB_DIM = 128
PEER_TOPK = 16
SB_LOG_UNDERFLOW = -104.0

_NT = (((1,), (1,)), ((), ()))
_TN = (((0,), (0,)), ((), ()))


def _round_up(n, m):
    return -(-n // m) * m


def _params(*sem):
    return pltpu.CompilerParams(dimension_semantics=sem, vmem_limit_bytes=V7X_VMEM_LIMIT_BYTES)


def _row_tile(rows):
    for t in (1280, 640, 256, 128, 64, 32, 16, 8):
        if rows % t == 0:
            return t
    raise ValueError(f"row count {rows} is not a multiple of 8")


def _col_tile(n):
    for t in (512, 384, 256, 128):
        if n % t == 0:
            return t
    raise ValueError(f"column count {n} is not a multiple of {LANE}")


def _split_bf16(x):
    hi = x.astype(BF16)
    return hi, (x - hi.astype(F32)).astype(BF16)


def _dot(a, b, dims=None):
    a, b = a.astype(BF16), b.astype(BF16)
    if dims is None:
        return jnp.dot(a, b, preferred_element_type=F32)
    return lax.dot_general(a, b, dims, preferred_element_type=F32)


def _dot_f32(a, b):
    a_hi, a_lo = _split_bf16(a)
    b_hi, b_lo = _split_bf16(b)
    return (jnp.dot(a_hi, b_hi, preferred_element_type=F32) + jnp.dot(a_hi, b_lo, preferred_element_type=F32)
            + jnp.dot(a_lo, b_hi, preferred_element_type=F32))


def _dot_exact_lhs(a_bf16, b):
    b_hi, b_lo = _split_bf16(b)
    return jnp.dot(a_bf16, b_hi, preferred_element_type=F32) + jnp.dot(a_bf16, b_lo, preferred_element_type=F32)


def _dot_exact_rhs(a, b_bf16):
    a_hi, a_lo = _split_bf16(a)
    return jnp.dot(a_hi, b_bf16, preferred_element_type=F32) + jnp.dot(a_lo, b_bf16, preferred_element_type=F32)


def _softplus(x):
    return jnp.maximum(x, 0.0) + jnp.log1p(jnp.exp(-jnp.abs(x)))


def _tri_masks(n):
    r = lax.broadcasted_iota(jnp.int32, (n, n), 0)
    c = lax.broadcasted_iota(jnp.int32, (n, n), 1)
    return r >= c, r > c, r == c


def _unit_lower_inverse(a, n):
    _, _, eye = _tri_masks(n)
    x = jnp.where(eye, 1.0, 0.0) + a
    p = a
    for _ in range(n.bit_length() - 2):
        p = _dot(p, p)
        x = x + _dot(x, p)
    return x


def _rmsnorm_kernel(x_ref, g_ref, o_ref):
    x = x_ref[...]
    ms = jnp.mean(x * x, axis=-1, keepdims=True)
    o_ref[...] = (x * lax.rsqrt(ms + NORM_EPS) * g_ref[...]).astype(o_ref.dtype)


def _rmsnorm(x, g, out_dtype):
    rows, d = x.shape
    tr = min(_row_tile(rows), 256)
    return pl.pallas_call(
        _rmsnorm_kernel,
        grid=(rows // tr,),
        in_specs=[pl.BlockSpec((tr, d), lambda i: (i, 0)), pl.BlockSpec((1, d), lambda i: (0, 0))],
        out_specs=pl.BlockSpec((tr, d), lambda i: (i, 0)),
        out_shape=jax.ShapeDtypeStruct((rows, d), out_dtype),
        compiler_params=_params("parallel"),
        name="rmsnorm",
    )(x, g.reshape(1, d).astype(F32))


def _mm_kernel(a_ref, b_ref, *rest, nt, has_res):
    o_ref = rest[-1]
    acc = lax.dot_general(a_ref[...], b_ref[...], _NT if nt else (((1,), (0,)), ((), ())),
                          preferred_element_type=F32)
    if has_res:
        acc = acc + rest[0][...]
    o_ref[...] = acc.astype(o_ref.dtype)


def _mm(a, b, *, nt=False, res=None, out_dtype=F32):
    m, k = a.shape
    n = b.shape[0] if nt else b.shape[1]
    tm, tn = _row_tile(m), _col_tile(n)
    in_specs = [
        pl.BlockSpec((tm, k), lambda i, j: (i, 0)),
        pl.BlockSpec((tn, k), lambda i, j: (j, 0)) if nt else pl.BlockSpec((k, tn), lambda i, j: (0, j)),
    ]
    args = [a, b]
    if res is not None:
        in_specs.append(pl.BlockSpec((tm, tn), lambda i, j: (i, j)))
        args.append(res)
    return pl.pallas_call(
        functools.partial(_mm_kernel, nt=nt, has_res=res is not None),
        grid=(m // tm, n // tn),
        in_specs=in_specs,
        out_specs=pl.BlockSpec((tm, tn), lambda i, j: (i, j)),
        out_shape=jax.ShapeDtypeStruct((m, n), out_dtype),
        compiler_params=_params("parallel", "arbitrary"),
        name="matmul",
    )(*args)


def _merge_kernel(o0_ref, o1_ref, o2_ref, wb_ref, g0_ref, g1_ref, g2_ref, out_ref):
    acc = None
    for o_ref, g_ref, i in ((o0_ref, g0_ref, 0), (o1_ref, g1_ref, 1), (o2_ref, g2_ref, 2)):
        term = jax.nn.sigmoid(g_ref[...]) * jnp.dot(o_ref[...], wb_ref[i], preferred_element_type=F32)
        acc = term if acc is None else acc + term
    out_ref[...] = acc.astype(out_ref.dtype)


def _merge(o_sb, o_rw, o_gdn, wb, gates):
    m, w = o_sb.shape
    d = wb.shape[2]
    tm, tn = _row_tile(m), _col_tile(d)
    nj = d // tn
    o_spec = pl.BlockSpec((tm, w), lambda i, j: (i, 0))

    def g_spec(b):
        return pl.BlockSpec((tm, tn), lambda i, j: (i, j + b * nj))

    return pl.pallas_call(
        _merge_kernel,
        grid=(m // tm, nj),
        in_specs=[o_spec, o_spec, o_spec, pl.BlockSpec((3, w, tn), lambda i, j: (0, 0, j)),
                  g_spec(0), g_spec(1), g_spec(2)],
        out_specs=pl.BlockSpec((tm, tn), lambda i, j: (i, j)),
        out_shape=jax.ShapeDtypeStruct((m, d), BF16),
        compiler_params=_params("parallel", "arbitrary"),
        name="branch_merge",
    )(o_sb, o_rw, o_gdn, wb, gates, gates, gates)


def _peer_hid_kernel(x_ref, u_ref, g_ref, o_ref):
    h = lax.dot_general(x_ref[...], u_ref[...], _NT, preferred_element_type=F32)
    hid = 0.5 * h * (1.0 + lax.erf(h * (2.0 ** -0.5)))
    o_ref[...] = (g_ref[...] * hid).astype(o_ref.dtype)


def _peer_hid(xn, u, g_dense):
    m, d = xn.shape
    e = u.shape[0]
    tm, te = _row_tile(m), _col_tile(e)
    return pl.pallas_call(
        _peer_hid_kernel,
        grid=(m // tm, e // te),
        in_specs=[pl.BlockSpec((tm, d), lambda i, j: (i, 0)), pl.BlockSpec((te, d), lambda i, j: (j, 0)),
                  pl.BlockSpec((tm, te), lambda i, j: (i, j))],
        out_specs=pl.BlockSpec((tm, te), lambda i, j: (i, j)),
        out_shape=jax.ShapeDtypeStruct((m, e), BF16),
        compiler_params=_params("parallel", "arbitrary"),
        name="peer_hidden",
    )(xn, u, g_dense)


def _sb_attn_kernel(q_ref, k_ref, v_ref, o_ref, *, bq, bk, q_off):
    i = pl.program_id(2)
    qpos0 = q_off + i * bq
    q = (q_ref[...] * (SB_DIM ** -0.5)).astype(BF16)
    qpos = qpos0 + lax.broadcasted_iota(jnp.int32, (bq, 1), 0)
    incl, _, _ = _tri_masks(bk)
    tri = incl.astype(BF16)
    kb_hi = jnp.maximum(qpos0 + bq - 2, 0) // bk

    def cond(c):
        kb, later, _ = c
        return jnp.logical_and(kb >= 0, jnp.max(later) > SB_LOG_UNDERFLOW)

    def body(c):
        kb, later, acc = c
        off = pl.multiple_of(kb * bk, bk)
        k = k_ref[pl.ds(off, bk), :]
        v = v_ref[pl.ds(off, bk), :]
        z = lax.dot_general(q, k, _NT, preferred_element_type=F32)
        kpos = off + lax.broadcasted_iota(jnp.int32, (1, bk), 1)
        mask = kpos < qpos
        lf = jnp.where(mask, -_softplus(z), 0.0)
        rev = _dot_exact_rhs(lf, tri)
        att = jnp.where(mask, jnp.exp(z + rev + later), 0.0)
        acc = acc + jnp.dot(att.astype(BF16), v, preferred_element_type=F32)
        return kb - 1, later + rev[:, 0:1], acc

    _, _, acc = lax.while_loop(
        cond, body, (kb_hi, jnp.zeros((bq, 1), F32), jnp.zeros((bq, SB_DIM), F32)))
    o_ref[...] = acc.astype(o_ref.dtype)


def _sb_attention(q, k, v, *, q_off):
    b, tq, w = q.shape
    tk = k.shape[1]
    heads = w // SB_DIM
    bq = min(_row_tile(tq), 128)
    bk = 128
    assert tk % bk == 0 and q_off + tq <= tk + 1
    kv_spec = pl.BlockSpec((None, tk, SB_DIM), lambda bi, h, i: (bi, 0, h))
    q_spec = pl.BlockSpec((None, bq, SB_DIM), lambda bi, h, i: (bi, i, h))
    return pl.pallas_call(
        functools.partial(_sb_attn_kernel, bq=bq, bk=bk, q_off=q_off),
        grid=(b, heads, tq // bq),
        in_specs=[q_spec, kv_spec, kv_spec],
        out_specs=q_spec,
        out_shape=jax.ShapeDtypeStruct((b, tq, w), BF16),
        compiler_params=_params("parallel", "parallel", "arbitrary"),
        name="sb_attention",
    )(q, k, v)


def _rwkv_kernel(cols_ref, hist_ref, h0_ref, mu_ref, w0_ref, w2_ref, a0_ref, a2_ref, g2_ref, kk_ref, ka_ref, rk_ref,
                 lnw_ref, lnb_ref, gsum_ref, o_ref, hfin_ref, ext_ref, h_ref, oacc_ref, *,
                 heads, dim, chunk, t_valid, lora_pad):
    c = pl.program_id(1)
    w = heads * dim
    n_dec, n_aaa, n_gate = lora_pad

    @pl.when(c == 0)
    def _():
        ext_ref[0:SUBLANE, :] = hist_ref[...]
        h_ref[...] = h0_ref[...]

    x = cols_ref[...]
    ext_ref[SUBLANE:SUBLANE + chunk, :] = x
    prev = ext_ref[SUBLANE - 1:SUBLANE - 1 + chunk, :]
    ext_ref[0:SUBLANE, :] = x[chunk - SUBLANE:chunk, :]
    xs = x + (prev - x) * mu_ref[...]
    r, k, v = xs[:, :w], xs[:, w:2 * w], xs[:, 2 * w:3 * w]
    xw = xs[:, 3 * w:3 * w + n_dec]
    xa = xs[:, 3 * w + n_dec:3 * w + n_dec + n_aaa]
    xg = xs[:, 3 * w + n_dec + n_aaa:3 * w + n_dec + n_aaa + n_gate]
    log_w = -_softplus(-(w0_ref[...] + _dot_f32(jnp.tanh(xw), w2_ref[...]))) - 0.5
    logd = -jnp.exp(log_w)
    a = jax.nn.sigmoid(a0_ref[...] + _dot_f32(xa, a2_ref[...]))
    gate = _dot_f32(jax.nn.sigmoid(xg), g2_ref[...])

    gsum = gsum_ref[...]
    kk = k * kk_ref[...]
    kk = kk / jnp.maximum(jnp.sqrt(_dot_exact_rhs(kk * kk, gsum)), 1e-12)
    k2 = k * (1.0 + (a - 1.0) * ka_ref[...])

    valid = (c * chunk + lax.broadcasted_iota(jnp.int32, (chunk, 1), 0)) < t_valid
    logd = jnp.where(valid, logd, 0.0)
    k_m = jnp.where(valid, k2, 0.0)
    v_m = jnp.where(valid, v, 0.0)
    a_m = jnp.where(valid, -kk, 0.0)
    b_m = jnp.where(valid, kk * a, 0.0)

    incl, strict, _ = _tri_masks(chunk)
    g_in = _dot_exact_lhs(incl.astype(BF16), logd)
    g_last = g_in[chunk - 1:chunk, :]
    e_neg = jnp.exp(-g_in)
    e_end = jnp.exp(g_last - g_in)
    r_t = r * jnp.exp(g_in)
    a_t = a_m * jnp.exp(g_in - logd)
    b_h, k_h = b_m * e_neg, k_m * e_neg
    b_end, k_end = b_m * e_end, k_m * e_end
    g_tot = jnp.exp(g_last)
    row2 = lax.broadcasted_iota(jnp.int32, (chunk, 2 * chunk), 0)
    col2 = lax.broadcasted_iota(jnp.int32, (chunk, 2 * chunk), 1)
    incl2 = row2 >= jnp.where(col2 >= chunk, col2 - chunk, col2)

    for h in range(heads):
        sl = slice(h * dim, (h + 1) * dim)
        a_h, v_h = a_t[:, sl], v_m[:, sl]
        a_ab = jnp.where(strict, _dot(a_h, b_h[:, sl], _NT), 0.0)
        a_ak = jnp.where(strict, _dot(a_h, k_h[:, sl], _NT), 0.0)
        a_rbk = jnp.where(incl2, _dot(r_t[:, sl], jnp.concatenate([b_h[:, sl], k_h[:, sl]], axis=0), _NT), 0.0)
        t_inv = _unit_lower_inverse(a_ab, chunk)
        w_mat = _dot(t_inv, a_h)
        u0 = _dot(t_inv, _dot(a_ak, v_h))
        state = h_ref[h]
        wr = _dot(jnp.concatenate([w_mat, r_t[:, sl]], axis=0), state)
        u = u0 + wr[:chunk]
        uv = jnp.concatenate([u, v_h], axis=0)
        oacc_ref[:, sl] = wr[chunk:] + _dot(a_rbk, uv)
        h_ref[h] = state * _row_to_col(g_tot[:, sl], dim) + _dot(jnp.concatenate([b_end[:, sl], k_end[:, sl]], axis=0), uv, _TN)

    o = oacc_ref[...]
    mean = _dot_exact_rhs(o, gsum) * (1.0 / dim)
    cen = o - mean
    var = _dot_exact_rhs(cen * cen, gsum) * (1.0 / dim)
    gn = cen * lax.rsqrt(var + RW_GN_EPS) * lnw_ref[...] + lnb_ref[...]
    bonus = _dot_exact_rhs(r * k2 * rk_ref[...], gsum) * v
    o_ref[...] = ((gn + bonus) * gate).astype(o_ref.dtype)

    @pl.when(c == pl.num_programs(1) - 1)
    def _():
        hfin_ref[...] = h_ref[...]


def _row_to_col(row, n):
    _, _, eye = _tri_masks(n)
    return jnp.sum(jnp.where(eye, row, 0.0), axis=1, keepdims=True)


def _rw_segments(lp):
    w = lp['rw_w0'].shape[0]
    loras = [lp['rw_w2'].shape[0], lp['rw_a2'].shape[0], lp['rw_g2'].shape[0]]
    return [(w, w)] * 3 + [(n, _round_up(n, LANE)) for n in loras]


def _pad_segments(x, segs):
    out, off = [], 0
    for n, n_pad in segs:
        piece = x[..., off:off + n]
        out.append(jnp.pad(piece, [(0, 0)] * (x.ndim - 1) + [(0, n_pad - n)]))
        off += n
    return jnp.concatenate(out, axis=-1)


def _unpad_segments(x, segs):
    out, off = [], 0
    for n, n_pad in segs:
        out.append(x[..., off:off + n])
        off += n_pad
    return jnp.concatenate(out, axis=-1)


def _pad_rows(w, n):
    return jnp.pad(w, ((0, n - w.shape[0]), (0, 0)))


def _rwkv_branch(cols, shift_buf, state, lp, t_valid, chunk):
    b, tp, n_pad = cols.shape
    heads, dim = lp['rw_r_k'].shape
    w = heads * dim
    segs = _rw_segments(lp)
    lora_pad = tuple(p for _, p in segs[3:])
    hist = jnp.pad(_pad_segments(shift_buf.astype(F32), segs), ((0, 0), (SUBLANE - 1, 0), (0, 0)))
    head_of = jnp.arange(w) // dim
    gsum = (head_of[:, None] == head_of[None, :]).astype(BF16)

    def row(p):
        return p.reshape(1, -1).astype(F32)

    consts = [row(_pad_segments(lp['rw_mu'], segs)), row(lp['rw_w0']), _pad_rows(lp['rw_w2'], lora_pad[0]),
              row(lp['rw_a0']), _pad_rows(lp['rw_a2'], lora_pad[1]), _pad_rows(lp['rw_g2'], lora_pad[2]),
              row(lp['rw_k_k']), row(lp['rw_k_a']), row(lp['rw_r_k']), row(lp['rw_ln_w']), row(lp['rw_ln_b']), gsum]
    const_specs = [pl.BlockSpec(a.shape, lambda bi, c: (0, 0)) for a in consts]
    state_spec = pl.BlockSpec((None, heads, dim, dim), lambda bi, c: (bi, 0, 0, 0))
    o, h_fin = pl.pallas_call(
        functools.partial(_rwkv_kernel, heads=heads, dim=dim, chunk=chunk, t_valid=t_valid, lora_pad=lora_pad),
        grid=(b, tp // chunk),
        in_specs=[pl.BlockSpec((None, chunk, n_pad), lambda bi, c: (bi, c, 0)),
                  pl.BlockSpec((None, SUBLANE, n_pad), lambda bi, c: (bi, 0, 0)), state_spec] + const_specs,
        out_specs=[pl.BlockSpec((None, chunk, w), lambda bi, c: (bi, c, 0)), state_spec],
        out_shape=[jax.ShapeDtypeStruct((b, tp, w), BF16), jax.ShapeDtypeStruct((b, heads, dim, dim), F32)],
        scratch_shapes=[pltpu.VMEM((chunk + SUBLANE, n_pad), F32), pltpu.VMEM((heads, dim, dim), F32),
                        pltpu.VMEM((chunk, w), F32)],
        compiler_params=_params("parallel", "arbitrary"),
        name="rwkv7",
    )(cols, hist, state.astype(F32), *consts)
    shift_new = _unpad_segments(cols[:, t_valid - 1:t_valid], segs)
    return o, shift_new, h_fin


def _gdn_kernel(qkv_ref, ab_ref, z_ref, hist_ref, s0_ref, cw_ref, alog_ref, dtb_ref, nw_ref, o_ref, sfin_ref,
                ext_ref, s_ref, *, heads, dim, chunk, t_valid, taps):
    c = pl.program_id(1)
    w = heads * dim

    @pl.when(c == 0)
    def _():
        ext_ref[0:SUBLANE, :] = hist_ref[...]
        s_ref[...] = s0_ref[...]

    x = qkv_ref[...]
    ext_ref[SUBLANE:SUBLANE + chunk, :] = x
    first = SUBLANE - (taps - 1)
    conv = ext_ref[first:first + chunk, :] * cw_ref[0:1, :]
    for i in range(1, taps):
        conv = conv + ext_ref[first + i:first + i + chunk, :] * cw_ref[i:i + 1, :]
    ext_ref[0:SUBLANE, :] = x[chunk - SUBLANE:chunk, :]
    act = conv * jax.nn.sigmoid(conv)

    valid = (c * chunk + lax.broadcasted_iota(jnp.int32, (chunk, 1), 0)) < t_valid
    ab = ab_ref[...]
    beta = jnp.where(valid, jax.nn.sigmoid(ab[:, :LANE]), 0.0)
    g = jnp.where(valid, -jnp.exp(alog_ref[...]) * _softplus(ab[:, LANE:] + dtb_ref[...]), 0.0)
    incl, strict, _ = _tri_masks(chunk)
    gc = _dot_exact_lhs(incl.astype(BF16), g)
    gc_t = jnp.concatenate([gc, jnp.zeros((LANE - chunk, LANE), F32)], axis=0).T

    for h in range(heads):
        sl = slice(h * dim, (h + 1) * dim)
        q = jnp.where(valid, act[:, sl], 0.0)
        k = jnp.where(valid, act[:, w + h * dim:w + (h + 1) * dim], 0.0)
        v = jnp.where(valid, act[:, 2 * w + h * dim:2 * w + (h + 1) * dim], 0.0)
        q = q * lax.rsqrt(jnp.sum(q * q, axis=-1, keepdims=True) + 1e-6) * (dim ** -0.5)
        k = k * lax.rsqrt(jnp.sum(k * k, axis=-1, keepdims=True) + 1e-6)
        b_h = beta[:, h:h + 1]
        g_col = gc[:, h:h + 1]
        g_row = gc_t[h:h + 1, :chunk]
        decay = jnp.exp(jnp.minimum(g_col - g_row, 0.0))
        kb = k * b_h
        qk = _dot(jnp.concatenate([kb, q], axis=0), k, _NT)
        m = jnp.where(strict, qk[:chunk] * decay, 0.0)
        attn = jnp.where(incl, qk[chunk:] * decay, 0.0)
        e_gc = jnp.exp(g_col)
        t_inv = _unit_lower_inverse(-m, chunk)
        u = _dot(t_inv, v * b_h)
        w_mat = _dot(t_inv, kb * e_gc)
        state = s_ref[h]
        ws = _dot(jnp.concatenate([w_mat, q * e_gc], axis=0), state)
        v_new = u - ws[:chunk]
        o = ws[chunk:] + _dot(attn, v_new)
        g_end = g_col[chunk - 1:chunk, :]
        s_ref[h] = state * jnp.exp(g_end) + _dot(k * jnp.exp(g_end - g_col), v_new, _TN)
        o = o * lax.rsqrt(jnp.mean(o * o, axis=-1, keepdims=True) + NORM_EPS) * nw_ref[...]
        z = z_ref[:, sl]
        o_ref[:, sl] = (o * (z * jax.nn.sigmoid(z))).astype(o_ref.dtype)

    @pl.when(c == pl.num_programs(1) - 1)
    def _():
        sfin_ref[...] = s_ref[...]


def _gdn_branch(qkv, ab, z, conv_buf, state, lp, t_valid, chunk):
    b, tp, w3 = qkv.shape
    heads = lp['gdn_a_log'].shape[0]
    dim = lp['gdn_norm_w'].shape[0]
    w = heads * dim
    taps = lp['gdn_conv_w'].shape[0]
    assert taps - 1 <= SUBLANE and t_valid >= taps - 1 and chunk <= LANE
    hist = jnp.pad(conv_buf.astype(F32), ((0, 0), (SUBLANE - (taps - 1), 0), (0, 0)))

    def lane_row(p):
        return jnp.pad(p.astype(F32), (0, LANE - p.shape[0])).reshape(1, LANE)

    consts = [lp['gdn_conv_w'].astype(F32), lane_row(lp['gdn_a_log']), lane_row(lp['gdn_dt_bias']),
              lp['gdn_norm_w'].reshape(1, dim).astype(F32)]
    const_specs = [pl.BlockSpec(a.shape, lambda bi, c: (0, 0)) for a in consts]
    state_spec = pl.BlockSpec((None, heads, dim, dim), lambda bi, c: (bi, 0, 0, 0))

    def rows(n):
        return pl.BlockSpec((None, chunk, n), lambda bi, c: (bi, c, 0))

    o, s_fin = pl.pallas_call(
        functools.partial(_gdn_kernel, heads=heads, dim=dim, chunk=chunk, t_valid=t_valid, taps=taps),
        grid=(b, tp // chunk),
        in_specs=[rows(w3), rows(2 * LANE), rows(w), pl.BlockSpec((None, SUBLANE, w3), lambda bi, c: (bi, 0, 0)),
                  state_spec] + const_specs,
        out_specs=[rows(w), state_spec],
        out_shape=[jax.ShapeDtypeStruct((b, tp, w), BF16), jax.ShapeDtypeStruct((b, heads, dim, dim), F32)],
        scratch_shapes=[pltpu.VMEM((chunk + SUBLANE, w3), F32), pltpu.VMEM((heads, dim, dim), F32)],
        compiler_params=_params("parallel", "arbitrary"),
        name="gated_deltanet",
    )(qkv, ab, z, hist, state.astype(F32), *consts)
    conv_new = qkv[:, t_valid - (taps - 1):t_valid]
    return o, conv_new, s_fin


def _peer_gate_kernel(qt_ref, khi_ref, klo_ref, g_ref, s_ref, sv_ref, cand_ref, fv_ref, e0_ref, e1_ref, gt_ref, *,
                      heads, n_keys, half):
    tt = qt_ref.shape[1]
    neg_inf = jnp.float32(-jnp.inf)
    key_iota = lax.broadcasted_iota(jnp.int32, (n_keys, tt), 0)

    for grp in range(2 * heads):
        p, h = divmod(grp, heads)
        q_hi, q_lo = _split_bf16(qt_ref[(2 * h + p) * half:(2 * h + p + 1) * half, :])
        s = (jnp.dot(khi_ref[grp], q_hi, preferred_element_type=F32)
             + jnp.dot(khi_ref[grp], q_lo, preferred_element_type=F32)
             + jnp.dot(klo_ref[grp], q_hi, preferred_element_type=F32))
        s_ref[grp * n_keys:(grp + 1) * n_keys, :] = s

        def extract(n, blk, grp=grp):
            m = jnp.max(blk, axis=0, keepdims=True)
            first = jnp.min(jnp.where(blk == m, key_iota, n_keys), axis=0, keepdims=True)
            sv_ref[n, grp:grp + 1, :] = m
            return jnp.where(key_iota == first, neg_inf, blk)

        lax.fori_loop(0, PEER_TOPK, extract, s)

    pairs = [(i, j) for i in range(PEER_TOPK) for j in range(PEER_TOPK) if (i + 1) * (j + 1) <= PEER_TOPK]
    for pos, (i, j) in enumerate(pairs):
        cand_ref[pos] = sv_ref[i, 0:heads, :] + sv_ref[j, heads:2 * heads, :]
    pair_iota = lax.broadcasted_iota(jnp.int32, (len(pairs), heads, tt), 0)

    def extract_pair(n, carry):
        c = cand_ref[...]
        m = jnp.max(c, axis=0)
        first = jnp.min(jnp.where(c == m[None], pair_iota, len(pairs)), axis=0)
        fv_ref[n] = m
        cand_ref[...] = jnp.where(pair_iota == first[None], neg_inf, c)
        return carry

    lax.fori_loop(0, PEER_TOPK, extract_pair, 0)
    fv = fv_ref[...]
    top, tau = fv[0], fv[PEER_TOPK - 1]
    inv_z = 1.0 / jnp.sum(jnp.exp(fv - top[None]), axis=0)

    m0, m1 = sv_ref[0, 0:heads, :], sv_ref[0, heads:2 * heads, :]
    for h in range(heads):
        s0 = s_ref[h * n_keys:(h + 1) * n_keys, :]
        s1 = s_ref[(heads + h) * n_keys:(heads + h + 1) * n_keys, :]
        e0_ref[h * n_keys:(h + 1) * n_keys, :] = jnp.exp(s0 - m0[h:h + 1, :]) * inv_z[h:h + 1, :]
        e1_ref[h * n_keys:(h + 1) * n_keys, :] = jnp.exp(s1 - m1[h:h + 1, :])

    def expert_rows(a, carry):
        acc = jnp.zeros((n_keys, tt), F32)
        for h in range(heads):
            s_ab = s_ref[pl.ds(h * n_keys + a, 1), :] + s_ref[(heads + h) * n_keys:(heads + h + 1) * n_keys, :]
            w_ab = e0_ref[pl.ds(h * n_keys + a, 1), :] * e1_ref[h * n_keys:(h + 1) * n_keys, :]
            acc = acc + jnp.where(s_ab >= tau[h:h + 1, :], w_ab, 0.0)
        gt_ref[pl.ds(pl.multiple_of(a * n_keys, n_keys), n_keys), :] = acc
        return carry

    lax.fori_loop(0, n_keys, expert_rows, 0)
    g_ref[...] = gt_ref[...].T


def _peer_gates(qt, subkeys):
    n_tok = qt.shape[1]
    heads, _, n_keys, half = subkeys.shape
    khi, klo = _split_bf16(jnp.transpose(subkeys.astype(F32), (1, 0, 2, 3)).reshape(2 * heads, n_keys, half))
    tt = 256 if n_tok % 256 == 0 else n_tok
    n_exp = n_keys * n_keys
    n_pairs = sum(PEER_TOPK // (i + 1) for i in range(PEER_TOPK))
    k_spec = pl.BlockSpec((2 * heads, n_keys, half), lambda i: (0, 0, 0))
    return pl.pallas_call(
        functools.partial(_peer_gate_kernel, heads=heads, n_keys=n_keys, half=half),
        grid=(n_tok // tt,),
        in_specs=[pl.BlockSpec((2 * heads * half, tt), lambda i: (0, i)), k_spec, k_spec],
        out_specs=pl.BlockSpec((tt, n_exp), lambda i: (i, 0)),
        out_shape=jax.ShapeDtypeStruct((n_tok, n_exp), F32),
        scratch_shapes=[pltpu.VMEM((2 * heads * n_keys, tt), F32), pltpu.VMEM((PEER_TOPK, 2 * heads, tt), F32),
                        pltpu.VMEM((n_pairs, heads, tt), F32), pltpu.VMEM((PEER_TOPK, heads, tt), F32),
                        pltpu.VMEM((heads * n_keys, tt), F32), pltpu.VMEM((heads * n_keys, tt), F32),
                        pltpu.VMEM((n_exp, tt), F32)],
        compiler_params=_params("parallel"),
        name="peer_gates",
    )(qt, khi, klo)


def _layer(x, lp, t_valid, sb_cache_k, sb_cache_v, rw_shift, rw_state, gdn_conv, gdn_state, prompt):
    b, tp, d = x.shape
    rows = b * tp
    x2 = x.reshape(rows, d)
    xn = _rmsnorm(x2, lp['norm_mix'], BF16)
    cols_sb = _mm(xn, lp['w_sb'])
    cols_rw = _mm(xn, lp['w_rw'])
    cols_gdn = _mm(xn, lp['w_gdn'])
    cols_ab = _mm(xn, lp['w_ab'])
    cols_z = _mm(xn, lp['w_z'])
    gates = _mm(xn, lp['w_gate'])

    sb_w = lp['w_sb'].shape[1] // 3
    q_sb = cols_sb[:, :sb_w].reshape(b, tp, sb_w)
    k_sb = cols_sb[:, sb_w:2 * sb_w].reshape(b, tp, sb_w)
    v_sb = cols_sb[:, 2 * sb_w:].reshape(b, tp, sb_w)
    if prompt:
        o_sb = _sb_attention(q_sb, k_sb.astype(BF16), v_sb.astype(BF16), q_off=0)
    else:
        past = sb_cache_k.shape[1]
        kpad = _round_up(past + tp, 128) - (past + tp)

        def keys(c, n):
            a = jnp.concatenate([c.reshape(b, past, sb_w), n], axis=1)
            return jnp.pad(a, ((0, 0), (0, kpad), (0, 0))).astype(BF16)

        o_sb = _sb_attention(q_sb, keys(sb_cache_k, k_sb), keys(sb_cache_v, v_sb), q_off=past)

    chunk = CHUNK if prompt else tp
    o_rw, rw_shift_new, rw_state_new = _rwkv_branch(cols_rw.reshape(b, tp, -1), rw_shift, rw_state, lp, t_valid, chunk)
    o_gdn, gdn_conv_new, gdn_state_new = _gdn_branch(
        cols_gdn.reshape(b, tp, -1), cols_ab.reshape(b, tp, -1), cols_z.reshape(b, tp, -1), gdn_conv, gdn_state, lp,
        t_valid, chunk)

    merged = _merge(o_sb.reshape(rows, sb_w), o_rw.reshape(rows, -1), o_gdn.reshape(rows, -1), lp['w_branch'], gates)
    x2 = _mm(merged, lp['w_out'], res=x2)

    xn2 = _rmsnorm(x2, lp['norm_ffn'], BF16)
    qt = _mm(lp['peer_wq_t'], xn2, nt=True)
    g_dense = _peer_gates(qt, lp['peer_subkeys'])
    gh = _peer_hid(xn2, lp['peer_u'], g_dense)
    x2 = _mm(gh, lp['peer_v'], res=x2)

    heads = sb_w // SB_DIM
    new = (k_sb[:, :t_valid].reshape(b, t_valid, heads, SB_DIM), v_sb[:, :t_valid].reshape(b, t_valid, heads, SB_DIM),
           rw_shift_new, rw_state_new, gdn_conv_new, gdn_state_new)
    return x2.reshape(b, tp, d), new


def kernel(x_prompt, x_sample, cache_sb_k, cache_sb_v, state_rw_shift, state_rw, state_gdn_conv, state_gdn, meta_tokens, norm_mix, w_in, rw_mu, rw_w0, rw_w2, rw_a0, rw_a2, rw_g2, rw_k_k, rw_k_a, rw_r_k, rw_ln_w, rw_ln_b, gdn_conv_w, gdn_a_log, gdn_dt_bias, gdn_norm_w, w_branch, w_out, norm_ffn, peer_wq, peer_subkeys, peer_u, peer_v, norm_final):
    depth = w_in.shape[0]
    b_p, seq, d = x_prompt.shape
    n_meta = meta_tokens.shape[0]
    sb_w = w_branch.shape[2]
    rw_cols = rw_mu.shape[1]
    gdn_heads, gdn_dim = gdn_a_log.shape[1], gdn_norm_w.shape[1]
    gdn_w = gdn_heads * gdn_dim
    rw_heads, rw_dim = rw_r_k.shape[1], rw_r_k.shape[2]
    gdn_taps = gdn_conv_w.shape[1]
    off_rw = 3 * sb_w
    off_gdn = off_rw + rw_cols
    off_ab = off_gdn + 3 * gdn_w
    off_z = off_ab + 2 * gdn_heads
    off_gate = off_z + gdn_w

    t_p = n_meta + seq
    tp_pad = _round_up(t_p, 256)
    hp = jnp.concatenate([jnp.broadcast_to(meta_tokens.astype(x_prompt.dtype)[None], (b_p, n_meta, d)), x_prompt],
                         axis=1)
    hp = jnp.pad(hp, ((0, 0), (0, tp_pad - t_p), (0, 0)))
    hs = x_sample
    t_s = x_sample.shape[1]

    p_rows, s_rows = [], []
    for l in range(depth):
        wl = w_in[l]
        lp = dict(
            norm_mix=norm_mix[l], norm_ffn=norm_ffn[l],
            rw_mu=rw_mu[l], rw_w0=rw_w0[l], rw_w2=rw_w2[l], rw_a0=rw_a0[l], rw_a2=rw_a2[l], rw_g2=rw_g2[l],
            rw_k_k=rw_k_k[l], rw_k_a=rw_k_a[l], rw_r_k=rw_r_k[l], rw_ln_w=rw_ln_w[l], rw_ln_b=rw_ln_b[l],
            gdn_conv_w=gdn_conv_w[l], gdn_a_log=gdn_a_log[l], gdn_dt_bias=gdn_dt_bias[l], gdn_norm_w=gdn_norm_w[l],
            w_branch=w_branch[l].astype(BF16), w_out=w_out[l].astype(BF16),
            peer_wq_t=peer_wq[l].T.astype(BF16), peer_subkeys=peer_subkeys[l],
            peer_u=peer_u[l].astype(BF16), peer_v=peer_v[l].astype(BF16))
        lp.update(
            w_sb=wl[:, :off_rw].astype(BF16),
            w_rw=_pad_segments(wl[:, off_rw:off_gdn], _rw_segments(lp)).astype(BF16),
            w_gdn=wl[:, off_gdn:off_ab].astype(BF16),
            w_ab=_pad_segments(wl[:, off_ab:off_z], [(gdn_heads, LANE)] * 2).astype(BF16),
            w_z=wl[:, off_z:off_gate].astype(BF16),
            w_gate=wl[:, off_gate:].astype(BF16))
        hp, p_new = _layer(
            hp, lp, t_p, None, None,
            jnp.zeros((b_p, 1, rw_cols), F32), jnp.zeros((b_p, rw_heads, rw_dim, rw_dim), F32),
            jnp.zeros((b_p, gdn_taps - 1, 3 * gdn_w), F32), jnp.zeros((b_p, gdn_heads, gdn_dim, gdn_dim), F32),
            True)
        hs, s_new = _layer(
            hs, lp, t_s, cache_sb_k[l], cache_sb_v[l], state_rw_shift[l], state_rw[l],
            state_gdn_conv[l], state_gdn[l], False)
        p_rows.append(p_new)
        s_rows.append(s_new)

    p_out = [jnp.stack(z) for z in zip(*p_rows)]
    s_out = [jnp.stack(z) for z in zip(*s_rows)]
    y_prompt = _rmsnorm(hp.reshape(b_p * tp_pad, d), norm_final, F32).reshape(b_p, tp_pad, d)[:, n_meta:t_p]
    b_s = hs.shape[0]
    y_sample = _rmsnorm(hs.reshape(b_s * t_s, d), norm_final, F32).reshape(b_s, t_s, d)
    return (y_prompt, y_sample, *p_out, *s_out)
```

```python
import functools

import jax
import jax.numpy as jnp
from jax import lax
from jax.experimental import pallas as pl
from jax.experimental.pallas import tpu as pltpu

F32 = jnp.float32
BF16 = jnp.bfloat16

V7X_VMEM_LIMIT_BYTES = 56 * 2**20
LANE = 128
SUBLANE = 8

NORM_EPS = 1e-6
RW_GN_EPS = 64e-5
CHUNK = 64
SB_DIM = 128
PEER_TOPK = 16
SB_LOG_UNDERFLOW = -104.0

_NT = (((1,), (1,)), ((), ()))
_TN = (((0,), (0,)), ((), ()))


def _round_up(n, m):
    return -(-n // m) * m


def _params(*sem):
    return pltpu.CompilerParams(dimension_semantics=sem, vmem_limit_bytes=V7X_VMEM_LIMIT_BYTES)


def _row_tile(rows):
    for t in (1280, 640, 256, 128, 64, 32, 16, 8):
        if rows % t == 0:
            return t
    raise ValueError(f"row count {rows} is not a multiple of 8")


def _col_tile(n):
    for t in (512, 384, 256, 128):
        if n % t == 0:
            return t
    raise ValueError(f"column count {n} is not a multiple of {LANE}")


def _split_bf16(x):
    hi = x.astype(BF16)
    return hi, (x - hi.astype(F32)).astype(BF16)


def _dot(a, b, dims=None):
    a, b = a.astype(BF16), b.astype(BF16)
    if dims is None:
        return jnp.dot(a, b, preferred_element_type=F32)
    return lax.dot_general(a, b, dims, preferred_element_type=F32)


def _dot_f32(a, b_hi, b_lo):
    a_hi, a_lo = _split_bf16(a)
    return (jnp.dot(a_hi, b_hi, preferred_element_type=F32) + jnp.dot(a_hi, b_lo, preferred_element_type=F32)
            + jnp.dot(a_lo, b_hi, preferred_element_type=F32))


def _dot_exact_lhs(a_bf16, b):
    b_hi, b_lo = _split_bf16(b)
    return jnp.dot(a_bf16, b_hi, preferred_element_type=F32) + jnp.dot(a_bf16, b_lo, preferred_element_type=F32)


def _dot_exact_rhs(a, b_bf16):
    a_hi, a_lo = _split_bf16(a)
    return jnp.dot(a_hi, b_bf16, preferred_element_type=F32) + jnp.dot(a_lo, b_bf16, preferred_element_type=F32)


def _softplus(x):
    return jnp.maximum(x, 0.0) + jnp.log1p(jnp.exp(-jnp.abs(x)))


def _tri_masks(n):
    r = lax.broadcasted_iota(jnp.int32, (n, n), 0)
    c = lax.broadcasted_iota(jnp.int32, (n, n), 1)
    return r >= c, r > c, r == c


def _unit_lower_inverses(mats, n):
    _, _, eye = _tri_masks(n)
    xs = [jnp.where(eye, 1.0, 0.0) + a for a in mats]
    ps = list(mats)
    for _ in range(n.bit_length() - 2):
        ps = [_dot(p, p) for p in ps]
        xs = [x + _dot(x, p) for x, p in zip(xs, ps)]
    return xs


def _rmsnorm_kernel(x_ref, g_ref, o_ref):
    x = x_ref[...]
    ms = jnp.mean(x * x, axis=-1, keepdims=True)
    o_ref[...] = (x * lax.rsqrt(ms + NORM_EPS) * g_ref[...]).astype(o_ref.dtype)


def _rmsnorm(x, g, out_dtype):
    rows, d = x.shape
    tr = min(_row_tile(rows), 256)
    return pl.pallas_call(
        _rmsnorm_kernel,
        grid=(rows // tr,),
        in_specs=[pl.BlockSpec((tr, d), lambda i: (i, 0)), pl.BlockSpec((1, d), lambda i: (0, 0))],
        out_specs=pl.BlockSpec((tr, d), lambda i: (i, 0)),
        out_shape=jax.ShapeDtypeStruct((rows, d), out_dtype),
        compiler_params=_params("parallel"),
        name="rmsnorm",
    )(x, g.reshape(1, d).astype(F32))


def _mm_kernel(a_ref, b_ref, *rest, nt, has_res):
    o_ref = rest[-1]
    acc = lax.dot_general(a_ref[...], b_ref[...], _NT if nt else (((1,), (0,)), ((), ())),
                          preferred_element_type=F32)
    if has_res:
        acc = acc + rest[0][...]
    o_ref[...] = acc.astype(o_ref.dtype)


def _mm(a, b, *, nt=False, res=None, out_dtype=F32):
    m, k = a.shape
    n = b.shape[0] if nt else b.shape[1]
    tm, tn = _row_tile(m), _col_tile(n)
    in_specs = [
        pl.BlockSpec((tm, k), lambda i, j: (i, 0)),
        pl.BlockSpec((tn, k), lambda i, j: (j, 0)) if nt else pl.BlockSpec((k, tn), lambda i, j: (0, j)),
    ]
    args = [a, b]
    if res is not None:
        in_specs.append(pl.BlockSpec((tm, tn), lambda i, j: (i, j)))
        args.append(res)
    return pl.pallas_call(
        functools.partial(_mm_kernel, nt=nt, has_res=res is not None),
        grid=(m // tm, n // tn),
        in_specs=in_specs,
        out_specs=pl.BlockSpec((tm, tn), lambda i, j: (i, j)),
        out_shape=jax.ShapeDtypeStruct((m, n), out_dtype),
        compiler_params=_params("parallel", "arbitrary"),
        name="matmul",
    )(*args)


def _merge_kernel(o0_ref, o1_ref, o2_ref, wb_ref, g0_ref, g1_ref, g2_ref, out_ref):
    acc = None
    for o_ref, g_ref, i in ((o0_ref, g0_ref, 0), (o1_ref, g1_ref, 1), (o2_ref, g2_ref, 2)):
        term = jax.nn.sigmoid(g_ref[...]) * jnp.dot(o_ref[...], wb_ref[i], preferred_element_type=F32)
        acc = term if acc is None else acc + term
    out_ref[...] = acc.astype(out_ref.dtype)


def _merge(o_sb, o_rw, o_gdn, wb, gates):
    m, w = o_sb.shape
    d = wb.shape[2]
    tm, tn = _row_tile(m), _col_tile(d)
    nj = d // tn
    o_spec = pl.BlockSpec((tm, w), lambda i, j: (i, 0))

    def g_spec(b):
        return pl.BlockSpec((tm, tn), lambda i, j: (i, j + b * nj))

    return pl.pallas_call(
        _merge_kernel,
        grid=(m // tm, nj),
        in_specs=[o_spec, o_spec, o_spec, pl.BlockSpec((3, w, tn), lambda i, j: (0, 0, j)),
                  g_spec(0), g_spec(1), g_spec(2)],
        out_specs=pl.BlockSpec((tm, tn), lambda i, j: (i, j)),
        out_shape=jax.ShapeDtypeStruct((m, d), BF16),
        compiler_params=_params("parallel", "arbitrary"),
        name="branch_merge",
    )(o_sb, o_rw, o_gdn, wb, gates, gates, gates)


def _peer_hid_kernel(x_ref, u_ref, g_ref, o_ref):
    h = lax.dot_general(x_ref[...], u_ref[...], _NT, preferred_element_type=F32)
    hid = 0.5 * h * (1.0 + lax.erf(h * (2.0 ** -0.5)))
    o_ref[...] = (g_ref[...] * hid).astype(o_ref.dtype)


def _peer_hid(xn, u, g_dense):
    m, d = xn.shape
    e = u.shape[0]
    tm, te = _row_tile(m), _col_tile(e)
    return pl.pallas_call(
        _peer_hid_kernel,
        grid=(m // tm, e // te),
        in_specs=[pl.BlockSpec((tm, d), lambda i, j: (i, 0)), pl.BlockSpec((te, d), lambda i, j: (j, 0)),
                  pl.BlockSpec((tm, te), lambda i, j: (i, j))],
        out_specs=pl.BlockSpec((tm, te), lambda i, j: (i, j)),
        out_shape=jax.ShapeDtypeStruct((m, e), BF16),
        compiler_params=_params("parallel", "arbitrary"),
        name="peer_hidden",
    )(xn, u, g_dense)


def _sb_attn_kernel(q_ref, k_ref, v_ref, o_ref, *, bq, bk, q_off, group):
    i = pl.program_id(2)
    qpos0 = q_off + i * bq
    hs = range(group)
    sls = [slice(h * SB_DIM, (h + 1) * SB_DIM) for h in hs]
    qs = [(q_ref[:, sl] * (SB_DIM ** -0.5)).astype(BF16) for sl in sls]
    qpos = qpos0 + lax.broadcasted_iota(jnp.int32, (bq, 1), 0)
    incl, _, _ = _tri_masks(bk)
    tri = incl.astype(BF16)
    kb_hi = jnp.maximum(qpos0 + bq - 2, 0) // bk

    def cond(c):
        kb, later, _ = c
        return jnp.logical_and(kb >= 0, jnp.max(functools.reduce(jnp.maximum, later)) > SB_LOG_UNDERFLOW)

    def body(c):
        kb, later, acc = c
        off = pl.multiple_of(kb * bk, bk)
        mask = (off + lax.broadcasted_iota(jnp.int32, (1, bk), 1)) < qpos
        zs = [lax.dot_general(qs[h], k_ref[pl.ds(off, bk), sls[h]], _NT, preferred_element_type=F32) for h in hs]
        lfs = [jnp.where(mask, -_softplus(z), 0.0) for z in zs]
        revs = [_dot_exact_rhs(lf, tri) for lf in lfs]
        atts = [jnp.where(mask, jnp.exp(zs[h] + revs[h] + later[h]), 0.0).astype(BF16) for h in hs]
        acc = tuple(acc[h] + jnp.dot(atts[h], v_ref[pl.ds(off, bk), sls[h]], preferred_element_type=F32) for h in hs)
        return kb - 1, tuple(later[h] + revs[h][:, 0:1] for h in hs), acc

    zeros = lambda n: tuple(jnp.zeros((bq, n), F32) for _ in hs)
    _, _, acc = lax.while_loop(cond, body, (kb_hi, zeros(1), zeros(SB_DIM)))
    for h in hs:
        o_ref[:, sls[h]] = acc[h].astype(o_ref.dtype)


def _sb_attention(q, k, v, *, q_off):
    b, tq, w = q.shape
    tk = k.shape[1]
    heads = w // SB_DIM
    group = 2 if heads % 2 == 0 else 1
    bq = min(_row_tile(tq), 256)
    bk = 128
    assert tk % bk == 0 and q_off + tq <= tk + 1
    kv_spec = pl.BlockSpec((None, tk, group * SB_DIM), lambda bi, h, i: (bi, 0, h))
    q_spec = pl.BlockSpec((None, bq, group * SB_DIM), lambda bi, h, i: (bi, i, h))
    return pl.pallas_call(
        functools.partial(_sb_attn_kernel, bq=bq, bk=bk, q_off=q_off, group=group),
        grid=(b, heads // group, tq // bq),
        in_specs=[q_spec, kv_spec, kv_spec],
        out_specs=q_spec,
        out_shape=jax.ShapeDtypeStruct((b, tq, w), BF16),
        compiler_params=_params("parallel", "parallel", "arbitrary"),
        name="sb_attention",
    )(q, k, v)


def _rwkv_kernel(cols_ref, hist_ref, h0_ref, mu_ref, w0_ref, w2_ref, a0_ref, a2_ref, g2_ref, kk_ref, ka_ref, rk_ref,
                 lnw_ref, lnb_ref, gred_ref, gexp_ref, o_ref, hfin_ref, ext_ref, h_ref, oacc_ref, *,
                 heads, dim, chunk, t_valid, lora_pad):
    c = pl.program_id(1)
    w = heads * dim
    n_dec, n_aaa, n_gate = lora_pad

    @pl.when(c == 0)
    def _():
        ext_ref[0:SUBLANE, :] = hist_ref[...]
        h_ref[...] = h0_ref[...]

    x = cols_ref[...]
    ext_ref[SUBLANE:SUBLANE + chunk, :] = x
    prev = ext_ref[SUBLANE - 1:SUBLANE - 1 + chunk, :]
    ext_ref[0:SUBLANE, :] = x[chunk - SUBLANE:chunk, :]
    xs = x + (prev - x) * mu_ref[...]
    r, k, v = xs[:, :w], xs[:, w:2 * w], xs[:, 2 * w:3 * w]
    xw = xs[:, 3 * w:3 * w + n_dec]
    xa = xs[:, 3 * w + n_dec:3 * w + n_dec + n_aaa]
    xg = xs[:, 3 * w + n_dec + n_aaa:3 * w + n_dec + n_aaa + n_gate]
    log_w = -_softplus(-(w0_ref[...] + _dot_f32(jnp.tanh(xw), w2_ref[0], w2_ref[1]))) - 0.5
    logd = -jnp.exp(log_w)
    a = jax.nn.sigmoid(a0_ref[...] + _dot_f32(xa, a2_ref[0], a2_ref[1]))
    gate = _dot_f32(jax.nn.sigmoid(xg), g2_ref[0], g2_ref[1])

    def head_sums(t):
        return _dot_exact_rhs(_dot_exact_rhs(t, gred_ref[...]), gexp_ref[...])

    kk = k * kk_ref[...]
    k2 = k * (1.0 + (a - 1.0) * ka_ref[...])
    sums = head_sums(jnp.concatenate([kk * kk, r * k2 * rk_ref[...]], axis=0))
    kk = kk / jnp.maximum(jnp.sqrt(sums[:chunk]), 1e-12)
    bonus = sums[chunk:] * v

    valid = (c * chunk + lax.broadcasted_iota(jnp.int32, (chunk, 1), 0)) < t_valid
    logd = jnp.where(valid, logd, 0.0)
    k_m = jnp.where(valid, k2, 0.0)
    v_m = jnp.where(valid, v, 0.0)
    a_m = jnp.where(valid, -kk, 0.0)
    b_m = jnp.where(valid, kk * a, 0.0)

    incl, strict, _ = _tri_masks(chunk)
    g_in = _dot_exact_lhs(incl.astype(BF16), logd)
    g_last = g_in[chunk - 1:chunk, :]
    e_neg = jnp.exp(-g_in)
    e_end = jnp.exp(g_last - g_in)
    r_t = r * jnp.exp(g_in)
    a_t = a_m * jnp.exp(g_in - logd)
    b_h, k_h = b_m * e_neg, k_m * e_neg
    b_end, k_end = b_m * e_end, k_m * e_end
    g_tot = jnp.exp(g_last)
    row2 = lax.broadcasted_iota(jnp.int32, (chunk, 2 * chunk), 0)
    col2 = lax.broadcasted_iota(jnp.int32, (chunk, 2 * chunk), 1)
    incl2 = row2 >= jnp.where(col2 >= chunk, col2 - chunk, col2)

    hs = range(heads)
    sls = [slice(h * dim, (h + 1) * dim) for h in hs]
    a_hs = [a_t[:, sl] for sl in sls]
    v_hs = [v_m[:, sl] for sl in sls]
    r_hs = [r_t[:, sl] for sl in sls]
    bk_hs = [jnp.concatenate([b_h[:, sl], k_h[:, sl]], axis=0) for sl in sls]
    a_ab = [jnp.where(strict, _dot(a_hs[h], b_h[:, sls[h]], _NT), 0.0) for h in hs]
    a_ak = [jnp.where(strict, _dot(a_hs[h], k_h[:, sls[h]], _NT), 0.0) for h in hs]
    a_rbk = [jnp.where(incl2, _dot(r_hs[h], bk_hs[h], _NT), 0.0) for h in hs]
    akv = [_dot(a_ak[h], v_hs[h]) for h in hs]
    t_inv = _unit_lower_inverses(a_ab, chunk)
    w_mat = [_dot(t_inv[h], a_hs[h]) for h in hs]
    u0 = [_dot(t_inv[h], akv[h]) for h in hs]
    state = [h_ref[h] for h in hs]
    wr = [_dot(jnp.concatenate([w_mat[h], r_hs[h]], axis=0), state[h]) for h in hs]
    uv = [jnp.concatenate([u0[h] + wr[h][:chunk], v_hs[h]], axis=0) for h in hs]
    for h in hs:
        oacc_ref[:, sls[h]] = wr[h][chunk:] + _dot(a_rbk[h], uv[h])
    for h in hs:
        end_h = jnp.concatenate([b_end[:, sls[h]], k_end[:, sls[h]]], axis=0)
        h_ref[h] = state[h] * _row_to_col(g_tot[:, sls[h]], dim) + _dot(end_h, uv[h], _TN)

    o = oacc_ref[...]
    cen = o - head_sums(o) * (1.0 / dim)
    var = head_sums(cen * cen) * (1.0 / dim)
    gn = cen * lax.rsqrt(var + RW_GN_EPS) * lnw_ref[...] + lnb_ref[...]
    o_ref[...] = ((gn + bonus) * gate).astype(o_ref.dtype)

    @pl.when(c == pl.num_programs(1) - 1)
    def _():
        hfin_ref[...] = h_ref[...]


def _row_to_col(row, n):
    _, _, eye = _tri_masks(n)
    return jnp.sum(jnp.where(eye, row, 0.0), axis=1, keepdims=True)


def _rw_segments(lp):
    w = lp['rw_w0'].shape[0]
    loras = [lp['rw_w2'].shape[0], lp['rw_a2'].shape[0], lp['rw_g2'].shape[0]]
    return [(w, w)] * 3 + [(n, _round_up(n, LANE)) for n in loras]


def _pad_segments(x, segs):
    out, off = [], 0
    for n, n_pad in segs:
        piece = x[..., off:off + n]
        out.append(jnp.pad(piece, [(0, 0)] * (x.ndim - 1) + [(0, n_pad - n)]))
        off += n
    return jnp.concatenate(out, axis=-1)


def _unpad_segments(x, segs):
    out, off = [], 0
    for n, n_pad in segs:
        out.append(x[..., off:off + n])
        off += n_pad
    return jnp.concatenate(out, axis=-1)


def _pad_rows(w, n):
    return jnp.pad(w, ((0, n - w.shape[0]), (0, 0)))


def _rwkv_branch(cols, shift_buf, state, lp, t_valid, chunk):
    b, tp, n_pad = cols.shape
    heads, dim = lp['rw_r_k'].shape
    w = heads * dim
    segs = _rw_segments(lp)
    lora_pad = tuple(p for _, p in segs[3:])
    hist = jnp.pad(_pad_segments(shift_buf.astype(F32), segs), ((0, 0), (SUBLANE - 1, 0), (0, 0)))
    assert heads <= LANE
    gred = (jnp.arange(w)[:, None] // dim == jnp.arange(LANE)[None, :]).astype(BF16)
    gexp = gred.T

    def row(p):
        return p.reshape(1, -1).astype(F32)

    def lora(p, n):
        return jnp.stack(_split_bf16(_pad_rows(p.astype(F32), n)))

    consts = [row(_pad_segments(lp['rw_mu'], segs)), row(lp['rw_w0']), lora(lp['rw_w2'], lora_pad[0]),
              row(lp['rw_a0']), lora(lp['rw_a2'], lora_pad[1]), lora(lp['rw_g2'], lora_pad[2]),
              row(lp['rw_k_k']), row(lp['rw_k_a']), row(lp['rw_r_k']), row(lp['rw_ln_w']), row(lp['rw_ln_b']),
              gred, gexp]
    const_specs = [pl.BlockSpec(a.shape, lambda bi, c, nd=a.ndim: (0,) * nd) for a in consts]
    state_spec = pl.BlockSpec((None, heads, dim, dim), lambda bi, c: (bi, 0, 0, 0))
    o, h_fin = pl.pallas_call(
        functools.partial(_rwkv_kernel, heads=heads, dim=dim, chunk=chunk, t_valid=t_valid, lora_pad=lora_pad),
        grid=(b, tp // chunk),
        in_specs=[pl.BlockSpec((None, chunk, n_pad), lambda bi, c: (bi, c, 0)),
                  pl.BlockSpec((None, SUBLANE, n_pad), lambda bi, c: (bi, 0, 0)), state_spec] + const_specs,
        out_specs=[pl.BlockSpec((None, chunk, w), lambda bi, c: (bi, c, 0)), state_spec],
        out_shape=[jax.ShapeDtypeStruct((b, tp, w), BF16), jax.ShapeDtypeStruct((b, heads, dim, dim), F32)],
        scratch_shapes=[pltpu.VMEM((chunk + SUBLANE, n_pad), F32), pltpu.VMEM((heads, dim, dim), F32),
                        pltpu.VMEM((chunk, w), F32)],
        compiler_params=_params("parallel", "arbitrary"),
        name="rwkv7",
    )(cols, hist, state.astype(F32), *consts)
    shift_new = _unpad_segments(cols[:, t_valid - 1:t_valid], segs)
    return o, shift_new, h_fin


def _gdn_kernel(qkv_ref, ab_ref, z_ref, hist_ref, s0_ref, cw_ref, alog_ref, dtb_ref, nw_ref, o_ref, sfin_ref,
                ext_ref, s_ref, *, heads, dim, chunk, t_valid, taps):
    c = pl.program_id(1)
    w = heads * dim

    @pl.when(c == 0)
    def _():
        ext_ref[0:SUBLANE, :] = hist_ref[...]
        s_ref[...] = s0_ref[...]

    x = qkv_ref[...]
    ext_ref[SUBLANE:SUBLANE + chunk, :] = x
    first = SUBLANE - (taps - 1)
    conv = ext_ref[first:first + chunk, :] * cw_ref[0:1, :]
    for i in range(1, taps):
        conv = conv + ext_ref[first + i:first + i + chunk, :] * cw_ref[i:i + 1, :]
    ext_ref[0:SUBLANE, :] = x[chunk - SUBLANE:chunk, :]
    act = conv * jax.nn.sigmoid(conv)

    valid = (c * chunk + lax.broadcasted_iota(jnp.int32, (chunk, 1), 0)) < t_valid
    ab = ab_ref[...]
    beta = jnp.where(valid, jax.nn.sigmoid(ab[:, :LANE]), 0.0)
    g = jnp.where(valid, -jnp.exp(alog_ref[...]) * _softplus(ab[:, LANE:] + dtb_ref[...]), 0.0)
    incl, strict, _ = _tri_masks(chunk)
    gc = _dot_exact_lhs(incl.astype(BF16), g)
    gc_t = jnp.concatenate([gc, jnp.zeros((LANE - chunk, LANE), F32)], axis=0).T

    hs = range(heads)
    qs, ks, vs, kbs, ms, attns, e_gcs, g_cols = [], [], [], [], [], [], [], []
    for h in hs:
        q = jnp.where(valid, act[:, h * dim:(h + 1) * dim], 0.0)
        k = jnp.where(valid, act[:, w + h * dim:w + (h + 1) * dim], 0.0)
        v = jnp.where(valid, act[:, 2 * w + h * dim:2 * w + (h + 1) * dim], 0.0)
        q = q * lax.rsqrt(jnp.sum(q * q, axis=-1, keepdims=True) + 1e-6) * (dim ** -0.5)
        k = k * lax.rsqrt(jnp.sum(k * k, axis=-1, keepdims=True) + 1e-6)
        b_h = beta[:, h:h + 1]
        g_col = gc[:, h:h + 1]
        decay = jnp.exp(jnp.minimum(g_col - gc_t[h:h + 1, :chunk], 0.0))
        kb = k * b_h
        qk = _dot(jnp.concatenate([kb, q], axis=0), k, _NT)
        ms.append(jnp.where(strict, qk[:chunk] * decay, 0.0))
        attns.append(jnp.where(incl, qk[chunk:] * decay, 0.0))
        qs.append(q), ks.append(k), vs.append(v * b_h), kbs.append(kb)
        e_gcs.append(jnp.exp(g_col)), g_cols.append(g_col)
    t_inv = _unit_lower_inverses([-m for m in ms], chunk)
    us = [_dot(t_inv[h], vs[h]) for h in hs]
    w_mats = [_dot(t_inv[h], kbs[h] * e_gcs[h]) for h in hs]
    states = [s_ref[h] for h in hs]
    wss = [_dot(jnp.concatenate([w_mats[h], qs[h] * e_gcs[h]], axis=0), states[h]) for h in hs]
    v_news = [us[h] - wss[h][:chunk] for h in hs]
    os_ = [wss[h][chunk:] + _dot(attns[h], v_news[h]) for h in hs]
    for h in hs:
        g_end = g_cols[h][chunk - 1:chunk, :]
        s_ref[h] = states[h] * jnp.exp(g_end) + _dot(ks[h] * jnp.exp(g_end - g_cols[h]), v_news[h], _TN)
    for h in hs:
        o = os_[h]
        o = o * lax.rsqrt(jnp.mean(o * o, axis=-1, keepdims=True) + NORM_EPS) * nw_ref[...]
        z = z_ref[:, h * dim:(h + 1) * dim]
        o_ref[:, h * dim:(h + 1) * dim] = (o * (z * jax.nn.sigmoid(z))).astype(o_ref.dtype)

    @pl.when(c == pl.num_programs(1) - 1)
    def _():
        sfin_ref[...] = s_ref[...]


def _gdn_branch(qkv, ab, z, conv_buf, state, lp, t_valid, chunk):
    b, tp, w3 = qkv.shape
    heads = lp['gdn_a_log'].shape[0]
    dim = lp['gdn_norm_w'].shape[0]
    w = heads * dim
    taps = lp['gdn_conv_w'].shape[0]
    assert taps - 1 <= SUBLANE and t_valid >= taps - 1 and chunk <= LANE
    hist = jnp.pad(conv_buf.astype(F32), ((0, 0), (SUBLANE - (taps - 1), 0), (0, 0)))

    def lane_row(p):
        return jnp.pad(p.astype(F32), (0, LANE - p.shape[0])).reshape(1, LANE)

    consts = [lp['gdn_conv_w'].astype(F32), lane_row(lp['gdn_a_log']), lane_row(lp['gdn_dt_bias']),
              lp['gdn_norm_w'].reshape(1, dim).astype(F32)]
    const_specs = [pl.BlockSpec(a.shape, lambda bi, c: (0, 0)) for a in consts]
    state_spec = pl.BlockSpec((None, heads, dim, dim), lambda bi, c: (bi, 0, 0, 0))

    def rows(n):
        return pl.BlockSpec((None, chunk, n), lambda bi, c: (bi, c, 0))

    o, s_fin = pl.pallas_call(
        functools.partial(_gdn_kernel, heads=heads, dim=dim, chunk=chunk, t_valid=t_valid, taps=taps),
        grid=(b, tp // chunk),
        in_specs=[rows(w3), rows(2 * LANE), rows(w), pl.BlockSpec((None, SUBLANE, w3), lambda bi, c: (bi, 0, 0)),
                  state_spec] + const_specs,
        out_specs=[rows(w), state_spec],
        out_shape=[jax.ShapeDtypeStruct((b, tp, w), BF16), jax.ShapeDtypeStruct((b, heads, dim, dim), F32)],
        scratch_shapes=[pltpu.VMEM((chunk + SUBLANE, w3), F32), pltpu.VMEM((heads, dim, dim), F32)],
        compiler_params=_params("parallel", "arbitrary"),
        name="gated_deltanet",
    )(qkv, ab, z, hist, state.astype(F32), *consts)
    conv_new = qkv[:, t_valid - (taps - 1):t_valid]
    return o, conv_new, s_fin


def _peer_gate_kernel(qt_ref, khi_ref, klo_ref, g_ref, s_ref, sv_ref, cand_ref, fv_ref, e0_ref, e1_ref, gt_ref, *,
                      heads, n_keys, half):
    tt = qt_ref.shape[1]
    neg_inf = jnp.float32(-jnp.inf)
    key_iota = lax.broadcasted_iota(jnp.int32, (n_keys, tt), 0)

    for grp in range(2 * heads):
        p, h = divmod(grp, heads)
        q_hi, q_lo = _split_bf16(qt_ref[(2 * h + p) * half:(2 * h + p + 1) * half, :])
        s = (jnp.dot(khi_ref[grp], q_hi, preferred_element_type=F32)
             + jnp.dot(khi_ref[grp], q_lo, preferred_element_type=F32)
             + jnp.dot(klo_ref[grp], q_hi, preferred_element_type=F32))
        s_ref[grp * n_keys:(grp + 1) * n_keys, :] = s

        def extract(n, blk, grp=grp):
            m = jnp.max(blk, axis=0, keepdims=True)
            first = jnp.min(jnp.where(blk == m, key_iota, n_keys), axis=0, keepdims=True)
            sv_ref[n, grp:grp + 1, :] = m
            return jnp.where(key_iota == first, neg_inf, blk)

        lax.fori_loop(0, PEER_TOPK, extract, s)

    pairs = [(i, j) for i in range(PEER_TOPK) for j in range(PEER_TOPK) if (i + 1) * (j + 1) <= PEER_TOPK]
    for pos, (i, j) in enumerate(pairs):
        cand_ref[pos] = sv_ref[i, 0:heads, :] + sv_ref[j, heads:2 * heads, :]
    pair_iota = lax.broadcasted_iota(jnp.int32, (len(pairs), heads, tt), 0)

    def extract_pair(n, carry):
        c = cand_ref[...]
        m = jnp.max(c, axis=0)
        first = jnp.min(jnp.where(c == m[None], pair_iota, len(pairs)), axis=0)
        fv_ref[n] = m
        cand_ref[...] = jnp.where(pair_iota == first[None], neg_inf, c)
        return carry

    lax.fori_loop(0, PEER_TOPK, extract_pair, 0)
    fv = fv_ref[...]
    top, tau = fv[0], fv[PEER_TOPK - 1]
    inv_z = 1.0 / jnp.sum(jnp.exp(fv - top[None]), axis=0)

    m0, m1 = sv_ref[0, 0:heads, :], sv_ref[0, heads:2 * heads, :]
    for h in range(heads):
        s0 = s_ref[h * n_keys:(h + 1) * n_keys, :]
        s1 = s_ref[(heads + h) * n_keys:(heads + h + 1) * n_keys, :]
        e0_ref[h * n_keys:(h + 1) * n_keys, :] = jnp.exp(s0 - m0[h:h + 1, :]) * inv_z[h:h + 1, :]
        e1_ref[h * n_keys:(h + 1) * n_keys, :] = jnp.exp(s1 - m1[h:h + 1, :])

    def expert_rows(a, carry):
        acc = jnp.zeros((n_keys, tt), F32)
        for h in range(heads):
            s_ab = s_ref[pl.ds(h * n_keys + a, 1), :] + s_ref[(heads + h) * n_keys:(heads + h + 1) * n_keys, :]
            w_ab = e0_ref[pl.ds(h * n_keys + a, 1), :] * e1_ref[h * n_keys:(h + 1) * n_keys, :]
            acc = acc + jnp.where(s_ab >= tau[h:h + 1, :], w_ab, 0.0)
        gt_ref[pl.ds(pl.multiple_of(a * n_keys, n_keys), n_keys), :] = acc
        return carry

    lax.fori_loop(0, n_keys, expert_rows, 0)
    g_ref[...] = gt_ref[...].T


def _peer_gates(qt, subkeys):
    n_tok = qt.shape[1]
    heads, _, n_keys, half = subkeys.shape
    khi, klo = _split_bf16(jnp.transpose(subkeys.astype(F32), (1, 0, 2, 3)).reshape(2 * heads, n_keys, half))
    tt = 256 if n_tok % 256 == 0 else n_tok
    n_exp = n_keys * n_keys
    n_pairs = sum(PEER_TOPK // (i + 1) for i in range(PEER_TOPK))
    k_spec = pl.BlockSpec((2 * heads, n_keys, half), lambda i: (0, 0, 0))
    return pl.pallas_call(
        functools.partial(_peer_gate_kernel, heads=heads, n_keys=n_keys, half=half),
        grid=(n_tok // tt,),
        in_specs=[pl.BlockSpec((2 * heads * half, tt), lambda i: (0, i)), k_spec, k_spec],
        out_specs=pl.BlockSpec((tt, n_exp), lambda i: (i, 0)),
        out_shape=jax.ShapeDtypeStruct((n_tok, n_exp), F32),
        scratch_shapes=[pltpu.VMEM((2 * heads * n_keys, tt), F32), pltpu.VMEM((PEER_TOPK, 2 * heads, tt), F32),
                        pltpu.VMEM((n_pairs, heads, tt), F32), pltpu.VMEM((PEER_TOPK, heads, tt), F32),
                        pltpu.VMEM((heads * n_keys, tt), F32), pltpu.VMEM((heads * n_keys, tt), F32),
                        pltpu.VMEM((n_exp, tt), F32)],
        compiler_params=_params("parallel"),
        name="peer_gates",
    )(qt, khi, klo)


def _layer(x, lp, t_valid, sb_cache_k, sb_cache_v, rw_shift, rw_state, gdn_conv, gdn_state, prompt):
    b, tp, d = x.shape
    rows = b * tp
    x2 = x.reshape(rows, d)
    xn = _rmsnorm(x2, lp['norm_mix'], BF16)
    cols_sb = _mm(xn, lp['w_sb'])
    cols_rw = _mm(xn, lp['w_rw'])
    cols_gdn = _mm(xn, lp['w_gdn'])
    cols_ab = _mm(xn, lp['w_ab'])
    cols_z = _mm(xn, lp['w_z'])
    gates = _mm(xn, lp['w_gate'])

    sb_w = lp['w_sb'].shape[1] // 3
    q_sb = cols_sb[:, :sb_w].reshape(b, tp, sb_w)
    k_sb = cols_sb[:, sb_w:2 * sb_w].reshape(b, tp, sb_w)
    v_sb = cols_sb[:, 2 * sb_w:].reshape(b, tp, sb_w)
    if prompt:
        o_sb = _sb_attention(q_sb, k_sb.astype(BF16), v_sb.astype(BF16), q_off=0)
    else:
        past = sb_cache_k.shape[1]
        kpad = _round_up(past + tp, 128) - (past + tp)

        def keys(c, n):
            a = jnp.concatenate([c.reshape(b, past, sb_w), n], axis=1)
            return jnp.pad(a, ((0, 0), (0, kpad), (0, 0))).astype(BF16)

        o_sb = _sb_attention(q_sb, keys(sb_cache_k, k_sb), keys(sb_cache_v, v_sb), q_off=past)

    chunk = CHUNK if prompt else tp
    o_rw, rw_shift_new, rw_state_new = _rwkv_branch(cols_rw.reshape(b, tp, -1), rw_shift, rw_state, lp, t_valid, chunk)
    o_gdn, gdn_conv_new, gdn_state_new = _gdn_branch(
        cols_gdn.reshape(b, tp, -1), cols_ab.reshape(b, tp, -1), cols_z.reshape(b, tp, -1), gdn_conv, gdn_state, lp,
        t_valid, chunk)

    merged = _merge(o_sb.reshape(rows, sb_w), o_rw.reshape(rows, -1), o_gdn.reshape(rows, -1), lp['w_branch'], gates)
    x2 = _mm(merged, lp['w_out'], res=x2)

    xn2 = _rmsnorm(x2, lp['norm_ffn'], BF16)
    qt = _mm(lp['peer_wq_t'], xn2, nt=True)
    g_dense = _peer_gates(qt, lp['peer_subkeys'])
    gh = _peer_hid(xn2, lp['peer_u'], g_dense)
    x2 = _mm(gh, lp['peer_v'], res=x2)

    heads = sb_w // SB_DIM
    new = (k_sb[:, :t_valid].reshape(b, t_valid, heads, SB_DIM), v_sb[:, :t_valid].reshape(b, t_valid, heads, SB_DIM),
           rw_shift_new, rw_state_new, gdn_conv_new, gdn_state_new)
    return x2.reshape(b, tp, d), new


def kernel(x_prompt, x_sample, cache_sb_k, cache_sb_v, state_rw_shift, state_rw, state_gdn_conv, state_gdn, meta_tokens, norm_mix, w_in, rw_mu, rw_w0, rw_w2, rw_a0, rw_a2, rw_g2, rw_k_k, rw_k_a, rw_r_k, rw_ln_w, rw_ln_b, gdn_conv_w, gdn_a_log, gdn_dt_bias, gdn_norm_w, w_branch, w_out, norm_ffn, peer_wq, peer_subkeys, peer_u, peer_v, norm_final):
    depth = w_in.shape[0]
    b_p, seq, d = x_prompt.shape
    n_meta = meta_tokens.shape[0]
    sb_w = w_branch.shape[2]
    rw_cols = rw_mu.shape[1]
    gdn_heads, gdn_dim = gdn_a_log.shape[1], gdn_norm_w.shape[1]
    gdn_w = gdn_heads * gdn_dim
    rw_heads, rw_dim = rw_r_k.shape[1], rw_r_k.shape[2]
    gdn_taps = gdn_conv_w.shape[1]
    off_rw = 3 * sb_w
    off_gdn = off_rw + rw_cols
    off_ab = off_gdn + 3 * gdn_w
    off_z = off_ab + 2 * gdn_heads
    off_gate = off_z + gdn_w

    t_p = n_meta + seq
    tp_pad = _round_up(t_p, 256)
    hp = jnp.concatenate([jnp.broadcast_to(meta_tokens.astype(x_prompt.dtype)[None], (b_p, n_meta, d)), x_prompt],
                         axis=1)
    hp = jnp.pad(hp, ((0, 0), (0, tp_pad - t_p), (0, 0)))
    hs = x_sample
    t_s = x_sample.shape[1]

    p_rows, s_rows = [], []
    for l in range(depth):
        wl = w_in[l]
        lp = dict(
            norm_mix=norm_mix[l], norm_ffn=norm_ffn[l],
            rw_mu=rw_mu[l], rw_w0=rw_w0[l], rw_w2=rw_w2[l], rw_a0=rw_a0[l], rw_a2=rw_a2[l], rw_g2=rw_g2[l],
            rw_k_k=rw_k_k[l], rw_k_a=rw_k_a[l], rw_r_k=rw_r_k[l], rw_ln_w=rw_ln_w[l], rw_ln_b=rw_ln_b[l],
            gdn_conv_w=gdn_conv_w[l], gdn_a_log=gdn_a_log[l], gdn_dt_bias=gdn_dt_bias[l], gdn_norm_w=gdn_norm_w[l],
            w_branch=w_branch[l].astype(BF16), w_out=w_out[l].astype(BF16),
            peer_wq_t=peer_wq[l].T.astype(BF16), peer_subkeys=peer_subkeys[l],
            peer_u=peer_u[l].astype(BF16), peer_v=peer_v[l].astype(BF16))
        lp.update(
            w_sb=wl[:, :off_rw].astype(BF16),
            w_rw=_pad_segments(wl[:, off_rw:off_gdn], _rw_segments(lp)).astype(BF16),
            w_gdn=wl[:, off_gdn:off_ab].astype(BF16),
            w_ab=_pad_segments(wl[:, off_ab:off_z], [(gdn_heads, LANE)] * 2).astype(BF16),
            w_z=wl[:, off_z:off_gate].astype(BF16),
            w_gate=wl[:, off_gate:].astype(BF16))
        hp, p_new = _layer(
            hp, lp, t_p, None, None,
            jnp.zeros((b_p, 1, rw_cols), F32), jnp.zeros((b_p, rw_heads, rw_dim, rw_dim), F32),
            jnp.zeros((b_p, gdn_taps - 1, 3 * gdn_w), F32), jnp.zeros((b_p, gdn_heads, gdn_dim, gdn_dim), F32),
            True)
        hs, s_new = _layer(
            hs, lp, t_s, cache_sb_k[l], cache_sb_v[l], state_rw_shift[l], state_rw[l],
            state_gdn_conv[l], state_gdn[l], False)
        p_rows.append(p_new)
        s_rows.append(s_new)

    p_out = [jnp.stack(z) for z in zip(*p_rows)]
    s_out = [jnp.stack(z) for z in zip(*s_rows)]
    y_prompt = _rmsnorm(hp.reshape(b_p * tp_pad, d), norm_final, F32).reshape(b_p, tp_pad, d)[:, n_meta:t_p]
    b_s = hs.shape[0]
    y_sample = _rmsnorm(hs.reshape(b_s * t_s, d), norm_final, F32).reshape(b_s, t_s, d)
    return (y_prompt, y_sample, *p_out, *s_out)
```

```python
import functools

import jax
import jax.numpy as jnp
from jax import lax
from jax.experimental import pallas as pl
from jax.experimental.pallas import tpu as pltpu

F32 = jnp.float32
BF16 = jnp.bfloat16

V7X_VMEM_LIMIT_BYTES = 56 * 2**20
LANE = 128
SUBLANE = 8

NORM_EPS = 1e-6
RW_GN_EPS = 64e-5
CHUNK = 64
SB_DIM = 128
PEER_TOPK = 16
SB_LOG_UNDERFLOW = -104.0

_NT = (((1,), (1,)), ((), ()))
_TN = (((0,), (0,)), ((), ()))


def _round_up(n, m):
    return -(-n // m) * m


def _params(*sem):
    return pltpu.CompilerParams(dimension_semantics=sem, vmem_limit_bytes=V7X_VMEM_LIMIT_BYTES)


def _row_tile(rows):
    for t in (1280, 640, 256, 128, 64, 32, 16, 8):
        if rows % t == 0:
            return t
    raise ValueError(f"row count {rows} is not a multiple of 8")


def _col_tile(n):
    for t in (512, 384, 256, 128):
        if n % t == 0:
            return t
    raise ValueError(f"column count {n} is not a multiple of {LANE}")


def _split_bf16(x):
    hi = x.astype(BF16)
    return hi, (x - hi.astype(F32)).astype(BF16)


def _dot(a, b, dims=None):
    a, b = a.astype(BF16), b.astype(BF16)
    if dims is None:
        return jnp.dot(a, b, preferred_element_type=F32)
    return lax.dot_general(a, b, dims, preferred_element_type=F32)


def _dot_f32(a, b_hi, b_lo):
    a_hi, a_lo = _split_bf16(a)
    return (jnp.dot(a_hi, b_hi, preferred_element_type=F32) + jnp.dot(a_hi, b_lo, preferred_element_type=F32)
            + jnp.dot(a_lo, b_hi, preferred_element_type=F32))


def _dot_exact_lhs(a_bf16, b):
    b_hi, b_lo = _split_bf16(b)
    return jnp.dot(a_bf16, b_hi, preferred_element_type=F32) + jnp.dot(a_bf16, b_lo, preferred_element_type=F32)


def _dot_exact_rhs(a, b_bf16):
    a_hi, a_lo = _split_bf16(a)
    return jnp.dot(a_hi, b_bf16, preferred_element_type=F32) + jnp.dot(a_lo, b_bf16, preferred_element_type=F32)


def _softplus(x):
    return jnp.maximum(x, 0.0) + jnp.log1p(jnp.exp(-jnp.abs(x)))


def _tri_masks(n):
    r = lax.broadcasted_iota(jnp.int32, (n, n), 0)
    c = lax.broadcasted_iota(jnp.int32, (n, n), 1)
    return r >= c, r > c, r == c


def _unit_lower_inverses(mats, n):
    _, _, eye = _tri_masks(n)
    xs = [jnp.where(eye, 1.0, 0.0) + a for a in mats]
    ps = list(mats)
    for _ in range(n.bit_length() - 2):
        ps = [_dot(p, p) for p in ps]
        xs = [x + _dot(x, p) for x, p in zip(xs, ps)]
    return xs


def _rmsnorm_kernel(x_ref, g_ref, o_ref):
    x = x_ref[...]
    ms = jnp.mean(x * x, axis=-1, keepdims=True)
    o_ref[...] = (x * lax.rsqrt(ms + NORM_EPS) * g_ref[...]).astype(o_ref.dtype)


def _rmsnorm(x, g, out_dtype):
    rows, d = x.shape
    tr = min(_row_tile(rows), 256)
    return pl.pallas_call(
        _rmsnorm_kernel,
        grid=(rows // tr,),
        in_specs=[pl.BlockSpec((tr, d), lambda i: (i, 0)), pl.BlockSpec((1, d), lambda i: (0, 0))],
        out_specs=pl.BlockSpec((tr, d), lambda i: (i, 0)),
        out_shape=jax.ShapeDtypeStruct((rows, d), out_dtype),
        compiler_params=_params("parallel"),
        name="rmsnorm",
    )(x, g.reshape(1, d).astype(F32))


def _mm_kernel(a_ref, b_ref, *rest, nt, has_res):
    o_ref = rest[-1]
    acc = lax.dot_general(a_ref[...], b_ref[...], _NT if nt else (((1,), (0,)), ((), ())),
                          preferred_element_type=F32)
    if has_res:
        acc = acc + rest[0][...]
    o_ref[...] = acc.astype(o_ref.dtype)


def _mm(a, b, *, nt=False, res=None, out_dtype=F32):
    m, k = a.shape
    n = b.shape[0] if nt else b.shape[1]
    tm, tn = (_col_tile(m), _row_tile(n)) if nt else (_row_tile(m), _col_tile(n))
    in_specs = [
        pl.BlockSpec((tm, k), lambda i, j: (i, 0)),
        pl.BlockSpec((tn, k), lambda i, j: (j, 0)) if nt else pl.BlockSpec((k, tn), lambda i, j: (0, j)),
    ]
    args = [a, b]
    if res is not None:
        in_specs.append(pl.BlockSpec((tm, tn), lambda i, j: (i, j)))
        args.append(res)
    return pl.pallas_call(
        functools.partial(_mm_kernel, nt=nt, has_res=res is not None),
        grid=(m // tm, n // tn),
        in_specs=in_specs,
        out_specs=pl.BlockSpec((tm, tn), lambda i, j: (i, j)),
        out_shape=jax.ShapeDtypeStruct((m, n), out_dtype),
        compiler_params=_params("parallel", "arbitrary"),
        name="matmul",
    )(*args)


def _merge_kernel(o0_ref, o1_ref, o2_ref, wb_ref, g0_ref, g1_ref, g2_ref, out_ref):
    acc = None
    for o_ref, g_ref, i in ((o0_ref, g0_ref, 0), (o1_ref, g1_ref, 1), (o2_ref, g2_ref, 2)):
        term = jax.nn.sigmoid(g_ref[...]) * jnp.dot(o_ref[...], wb_ref[i], preferred_element_type=F32)
        acc = term if acc is None else acc + term
    out_ref[...] = acc.astype(out_ref.dtype)


def _merge(o_sb, o_rw, o_gdn, wb, gates):
    m, w = o_sb.shape
    d = wb.shape[2]
    tm, tn = _row_tile(m), _col_tile(d)
    nj = d // tn
    o_spec = pl.BlockSpec((tm, w), lambda i, j: (i, 0))

    def g_spec(b):
        return pl.BlockSpec((tm, tn), lambda i, j: (i, j + b * nj))

    return pl.pallas_call(
        _merge_kernel,
        grid=(m // tm, nj),
        in_specs=[o_spec, o_spec, o_spec, pl.BlockSpec((3, w, tn), lambda i, j: (0, 0, j)),
                  g_spec(0), g_spec(1), g_spec(2)],
        out_specs=pl.BlockSpec((tm, tn), lambda i, j: (i, j)),
        out_shape=jax.ShapeDtypeStruct((m, d), BF16),
        compiler_params=_params("parallel", "arbitrary"),
        name="branch_merge",
    )(o_sb, o_rw, o_gdn, wb, gates, gates, gates)


def _peer_hid_kernel(x_ref, u_ref, g_ref, o_ref):
    h = lax.dot_general(x_ref[...], u_ref[...], _NT, preferred_element_type=F32)
    hid = 0.5 * h * (1.0 + lax.erf(h * (2.0 ** -0.5)))
    o_ref[...] = (g_ref[...] * hid).astype(o_ref.dtype)


def _peer_hid(xn, u, g_dense):
    m, d = xn.shape
    e = u.shape[0]
    tm, te = _row_tile(m), _col_tile(e)
    return pl.pallas_call(
        _peer_hid_kernel,
        grid=(m // tm, e // te),
        in_specs=[pl.BlockSpec((tm, d), lambda i, j: (i, 0)), pl.BlockSpec((te, d), lambda i, j: (j, 0)),
                  pl.BlockSpec((tm, te), lambda i, j: (i, j))],
        out_specs=pl.BlockSpec((tm, te), lambda i, j: (i, j)),
        out_shape=jax.ShapeDtypeStruct((m, e), BF16),
        compiler_params=_params("parallel", "arbitrary"),
        name="peer_hidden",
    )(xn, u, g_dense)


def _sb_attn_kernel(q_ref, k_ref, v_ref, o_ref, *, bq, bk, q_off, group):
    i = pl.program_id(2)
    qpos0 = q_off + i * bq
    hs = range(group)
    sls = [slice(h * SB_DIM, (h + 1) * SB_DIM) for h in hs]
    qs = [(q_ref[:, sl] * (SB_DIM ** -0.5)).astype(BF16) for sl in sls]
    qpos = qpos0 + lax.broadcasted_iota(jnp.int32, (bq, 1), 0)
    incl, _, _ = _tri_masks(bk)
    tri = incl.astype(BF16)
    kb_hi = jnp.maximum(qpos0 + bq - 2, 0) // bk

    def cond(c):
        kb, later, _ = c
        return jnp.logical_and(kb >= 0, jnp.max(functools.reduce(jnp.maximum, later)) > SB_LOG_UNDERFLOW)

    def body(c):
        kb, later, acc = c
        off = pl.multiple_of(kb * bk, bk)
        mask = (off + lax.broadcasted_iota(jnp.int32, (1, bk), 1)) < qpos
        zs = [lax.dot_general(qs[h], k_ref[pl.ds(off, bk), sls[h]], _NT, preferred_element_type=F32) for h in hs]
        lfs = [jnp.where(mask, -_softplus(z), 0.0) for z in zs]
        revs = [_dot_exact_rhs(lf, tri) for lf in lfs]
        atts = [jnp.where(mask, jnp.exp(zs[h] + revs[h] + later[h]), 0.0).astype(BF16) for h in hs]
        acc = tuple(acc[h] + jnp.dot(atts[h], v_ref[pl.ds(off, bk), sls[h]], preferred_element_type=F32) for h in hs)
        return kb - 1, tuple(later[h] + revs[h][:, 0:1] for h in hs), acc

    zeros = lambda n: tuple(jnp.zeros((bq, n), F32) for _ in hs)
    _, _, acc = lax.while_loop(cond, body, (kb_hi, zeros(1), zeros(SB_DIM)))
    for h in hs:
        o_ref[:, sls[h]] = acc[h].astype(o_ref.dtype)


def _sb_attention(q, k, v, *, q_off):
    b, tq, w = q.shape
    tk = k.shape[1]
    heads = w // SB_DIM
    group = next(g for g in (4, 2, 1) if heads % g == 0)
    bq = min(_row_tile(tq), 256)
    bk = 128
    assert tk % bk == 0 and q_off + tq <= tk + 1
    kv_spec = pl.BlockSpec((None, tk, group * SB_DIM), lambda bi, h, i: (bi, 0, h), pipeline_mode=pl.Buffered(1))
    q_spec = pl.BlockSpec((None, bq, group * SB_DIM), lambda bi, h, i: (bi, i, h))
    return pl.pallas_call(
        functools.partial(_sb_attn_kernel, bq=bq, bk=bk, q_off=q_off, group=group),
        grid=(b, heads // group, tq // bq),
        in_specs=[q_spec, kv_spec, kv_spec],
        out_specs=q_spec,
        out_shape=jax.ShapeDtypeStruct((b, tq, w), BF16),
        compiler_params=_params("parallel", "parallel", "arbitrary"),
        name="sb_attention",
    )(q, k, v)


def _rwkv_kernel(cols_ref, hist_ref, h0_ref, mu_ref, w0_ref, w2_ref, a0_ref, a2_ref, g2_ref, kk_ref, ka_ref, rk_ref,
                 lnw_ref, lnb_ref, gred_ref, gexp_ref, o_ref, hfin_ref, ext_ref, h_ref, oacc_ref, *,
                 heads, dim, chunk, t_valid, lora_pad):
    c = pl.program_id(1)
    w = heads * dim
    n_dec, n_aaa, n_gate = lora_pad

    @pl.when(c == 0)
    def _():
        ext_ref[0:SUBLANE, :] = hist_ref[...]
        h_ref[...] = h0_ref[...]

    x = cols_ref[...]
    ext_ref[SUBLANE:SUBLANE + chunk, :] = x
    prev = ext_ref[SUBLANE - 1:SUBLANE - 1 + chunk, :]
    ext_ref[0:SUBLANE, :] = x[chunk - SUBLANE:chunk, :]
    xs = x + (prev - x) * mu_ref[...]
    r, k, v = xs[:, :w], xs[:, w:2 * w], xs[:, 2 * w:3 * w]
    xw = xs[:, 3 * w:3 * w + n_dec]
    xa = xs[:, 3 * w + n_dec:3 * w + n_dec + n_aaa]
    xg = xs[:, 3 * w + n_dec + n_aaa:3 * w + n_dec + n_aaa + n_gate]
    log_w = -_softplus(-(w0_ref[...] + _dot_f32(jnp.tanh(xw), w2_ref[0], w2_ref[1]))) - 0.5
    logd = -jnp.exp(log_w)
    a = jax.nn.sigmoid(a0_ref[...] + _dot_f32(xa, a2_ref[0], a2_ref[1]))
    gate = _dot_f32(jax.nn.sigmoid(xg), g2_ref[0], g2_ref[1])

    def head_sums(t):
        return _dot_exact_rhs(_dot_exact_rhs(t, gred_ref[...]), gexp_ref[...])

    kk = k * kk_ref[...]
    k2 = k * (1.0 + (a - 1.0) * ka_ref[...])
    sums = head_sums(jnp.concatenate([kk * kk, r * k2 * rk_ref[...]], axis=0))
    kk = kk / jnp.maximum(jnp.sqrt(sums[:chunk]), 1e-12)
    bonus = sums[chunk:] * v

    valid = (c * chunk + lax.broadcasted_iota(jnp.int32, (chunk, 1), 0)) < t_valid
    logd = jnp.where(valid, logd, 0.0)
    k_m = jnp.where(valid, k2, 0.0)
    v_m = jnp.where(valid, v, 0.0)
    a_m = jnp.where(valid, -kk, 0.0)
    b_m = jnp.where(valid, kk * a, 0.0)

    incl, strict, _ = _tri_masks(chunk)
    g_in = _dot_exact_lhs(incl.astype(BF16), logd)
    g_last = g_in[chunk - 1:chunk, :]
    e_neg = jnp.exp(-g_in)
    e_end = jnp.exp(g_last - g_in)
    r_t = r * jnp.exp(g_in)
    a_t = a_m * jnp.exp(g_in - logd)
    b_h, k_h = b_m * e_neg, k_m * e_neg
    b_end, k_end = b_m * e_end, k_m * e_end
    g_tot = jnp.exp(g_last)
    row2 = lax.broadcasted_iota(jnp.int32, (chunk, 2 * chunk), 0)
    col2 = lax.broadcasted_iota(jnp.int32, (chunk, 2 * chunk), 1)
    incl2 = row2 >= jnp.where(col2 >= chunk, col2 - chunk, col2)

    hs = range(heads)
    sls = [slice(h * dim, (h + 1) * dim) for h in hs]
    a_hs = [a_t[:, sl] for sl in sls]
    v_hs = [v_m[:, sl] for sl in sls]
    r_hs = [r_t[:, sl] for sl in sls]
    bk_hs = [jnp.concatenate([b_h[:, sl], k_h[:, sl]], axis=0) for sl in sls]
    a_ab = [jnp.where(strict, _dot(a_hs[h], b_h[:, sls[h]], _NT), 0.0) for h in hs]
    a_ak = [jnp.where(strict, _dot(a_hs[h], k_h[:, sls[h]], _NT), 0.0) for h in hs]
    a_rbk = [jnp.where(incl2, _dot(r_hs[h], bk_hs[h], _NT), 0.0) for h in hs]
    akv = [_dot(a_ak[h], v_hs[h]) for h in hs]
    t_inv = _unit_lower_inverses(a_ab, chunk)
    w_mat = [_dot(t_inv[h], a_hs[h]) for h in hs]
    u0 = [_dot(t_inv[h], akv[h]) for h in hs]
    state = [h_ref[h] for h in hs]
    wr = [_dot(jnp.concatenate([w_mat[h], r_hs[h]], axis=0), state[h]) for h in hs]
    uv = [jnp.concatenate([u0[h] + wr[h][:chunk], v_hs[h]], axis=0) for h in hs]
    for h in hs:
        oacc_ref[:, sls[h]] = wr[h][chunk:] + _dot(a_rbk[h], uv[h])
    for h in hs:
        end_h = jnp.concatenate([b_end[:, sls[h]], k_end[:, sls[h]]], axis=0)
        h_ref[h] = state[h] * _row_to_col(g_tot[:, sls[h]], dim) + _dot(end_h, uv[h], _TN)

    o = oacc_ref[...]
    cen = o - head_sums(o) * (1.0 / dim)
    var = head_sums(cen * cen) * (1.0 / dim)
    gn = cen * lax.rsqrt(var + RW_GN_EPS) * lnw_ref[...] + lnb_ref[...]
    o_ref[...] = ((gn + bonus) * gate).astype(o_ref.dtype)

    @pl.when(c == pl.num_programs(1) - 1)
    def _():
        hfin_ref[...] = h_ref[...]


def _row_to_col(row, n):
    _, _, eye = _tri_masks(n)
    return jnp.sum(jnp.where(eye, row, 0.0), axis=1, keepdims=True)


def _rw_segments(lp):
    w = lp['rw_w0'].shape[0]
    loras = [lp['rw_w2'].shape[0], lp['rw_a2'].shape[0], lp['rw_g2'].shape[0]]
    return [(w, w)] * 3 + [(n, _round_up(n, LANE)) for n in loras]


def _pad_segments(x, segs):
    out, off = [], 0
    for n, n_pad in segs:
        piece = x[..., off:off + n]
        out.append(jnp.pad(piece, [(0, 0)] * (x.ndim - 1) + [(0, n_pad - n)]))
        off += n
    return jnp.concatenate(out, axis=-1)


def _unpad_segments(x, segs):
    out, off = [], 0
    for n, n_pad in segs:
        out.append(x[..., off:off + n])
        off += n_pad
    return jnp.concatenate(out, axis=-1)


def _pad_rows(w, n):
    return jnp.pad(w, ((0, n - w.shape[0]), (0, 0)))


def _rwkv_branch(cols, shift_buf, state, lp, t_valid, chunk):
    b, tp, n_pad = cols.shape
    heads, dim = lp['rw_r_k'].shape
    w = heads * dim
    segs = _rw_segments(lp)
    lora_pad = tuple(p for _, p in segs[3:])
    hist = jnp.pad(_pad_segments(shift_buf.astype(F32), segs), ((0, 0), (SUBLANE - 1, 0), (0, 0)))
    assert heads <= LANE
    gred = (jnp.arange(w)[:, None] // dim == jnp.arange(LANE)[None, :]).astype(BF16)
    gexp = gred.T

    def row(p):
        return p.reshape(1, -1).astype(F32)

    def lora(p, n):
        return jnp.stack(_split_bf16(_pad_rows(p.astype(F32), n)))

    consts = [row(_pad_segments(lp['rw_mu'], segs)), row(lp['rw_w0']), lora(lp['rw_w2'], lora_pad[0]),
              row(lp['rw_a0']), lora(lp['rw_a2'], lora_pad[1]), lora(lp['rw_g2'], lora_pad[2]),
              row(lp['rw_k_k']), row(lp['rw_k_a']), row(lp['rw_r_k']), row(lp['rw_ln_w']), row(lp['rw_ln_b']),
              gred, gexp]
    const_specs = [pl.BlockSpec(a.shape, lambda bi, c, nd=a.ndim: (0,) * nd) for a in consts]
    state_spec = pl.BlockSpec((None, heads, dim, dim), lambda bi, c: (bi, 0, 0, 0))
    o, h_fin = pl.pallas_call(
        functools.partial(_rwkv_kernel, heads=heads, dim=dim, chunk=chunk, t_valid=t_valid, lora_pad=lora_pad),
        grid=(b, tp // chunk),
        in_specs=[pl.BlockSpec((None, chunk, n_pad), lambda bi, c: (bi, c, 0)),
                  pl.BlockSpec((None, SUBLANE, n_pad), lambda bi, c: (bi, 0, 0)), state_spec] + const_specs,
        out_specs=[pl.BlockSpec((None, chunk, w), lambda bi, c: (bi, c, 0)), state_spec],
        out_shape=[jax.ShapeDtypeStruct((b, tp, w), BF16), jax.ShapeDtypeStruct((b, heads, dim, dim), F32)],
        scratch_shapes=[pltpu.VMEM((chunk + SUBLANE, n_pad), F32), pltpu.VMEM((heads, dim, dim), F32),
                        pltpu.VMEM((chunk, w), F32)],
        compiler_params=_params("parallel", "arbitrary"),
        name="rwkv7",
    )(cols, hist, state.astype(F32), *consts)
    shift_new = _unpad_segments(cols[:, t_valid - 1:t_valid], segs)
    return o, shift_new, h_fin


def _gdn_kernel(qkv_ref, ab_ref, z_ref, hist_ref, s0_ref, cw_ref, alog_ref, dtb_ref, nw_ref, o_ref, sfin_ref,
                ext_ref, s_ref, *, heads, dim, chunk, t_valid, taps):
    c = pl.program_id(1)
    w = heads * dim

    @pl.when(c == 0)
    def _():
        ext_ref[0:SUBLANE, :] = hist_ref[...]
        s_ref[...] = s0_ref[...]

    x = qkv_ref[...]
    ext_ref[SUBLANE:SUBLANE + chunk, :] = x
    first = SUBLANE - (taps - 1)
    conv = ext_ref[first:first + chunk, :] * cw_ref[0:1, :]
    for i in range(1, taps):
        conv = conv + ext_ref[first + i:first + i + chunk, :] * cw_ref[i:i + 1, :]
    ext_ref[0:SUBLANE, :] = x[chunk - SUBLANE:chunk, :]
    act = conv * jax.nn.sigmoid(conv)

    valid = (c * chunk + lax.broadcasted_iota(jnp.int32, (chunk, 1), 0)) < t_valid
    ab = ab_ref[...]
    beta = jnp.where(valid, jax.nn.sigmoid(ab[:, :LANE]), 0.0)
    g = jnp.where(valid, -jnp.exp(alog_ref[...]) * _softplus(ab[:, LANE:] + dtb_ref[...]), 0.0)
    incl, strict, _ = _tri_masks(chunk)
    gc = _dot_exact_lhs(incl.astype(BF16), g)
    gc_t = jnp.concatenate([gc, jnp.zeros((LANE - chunk, LANE), F32)], axis=0).T

    hs = range(heads)
    qs, ks, vs, kbs, ms, attns, e_gcs, g_cols = [], [], [], [], [], [], [], []
    for h in hs:
        q = jnp.where(valid, act[:, h * dim:(h + 1) * dim], 0.0)
        k = jnp.where(valid, act[:, w + h * dim:w + (h + 1) * dim], 0.0)
        v = jnp.where(valid, act[:, 2 * w + h * dim:2 * w + (h + 1) * dim], 0.0)
        q = q * lax.rsqrt(jnp.sum(q * q, axis=-1, keepdims=True) + 1e-6) * (dim ** -0.5)
        k = k * lax.rsqrt(jnp.sum(k * k, axis=-1, keepdims=True) + 1e-6)
        b_h = beta[:, h:h + 1]
        g_col = gc[:, h:h + 1]
        decay = jnp.exp(jnp.minimum(g_col - gc_t[h:h + 1, :chunk], 0.0))
        kb = k * b_h
        qk = _dot(jnp.concatenate([kb, q], axis=0), k, _NT)
        ms.append(jnp.where(strict, qk[:chunk] * decay, 0.0))
        attns.append(jnp.where(incl, qk[chunk:] * decay, 0.0))
        qs.append(q), ks.append(k), vs.append(v * b_h), kbs.append(kb)
        e_gcs.append(jnp.exp(g_col)), g_cols.append(g_col)
    t_inv = _unit_lower_inverses([-m for m in ms], chunk)
    us = [_dot(t_inv[h], vs[h]) for h in hs]
    w_mats = [_dot(t_inv[h], kbs[h] * e_gcs[h]) for h in hs]
    states = [s_ref[h] for h in hs]
    wss = [_dot(jnp.concatenate([w_mats[h], qs[h] * e_gcs[h]], axis=0), states[h]) for h in hs]
    v_news = [us[h] - wss[h][:chunk] for h in hs]
    os_ = [wss[h][chunk:] + _dot(attns[h], v_news[h]) for h in hs]
    for h in hs:
        g_end = g_cols[h][chunk - 1:chunk, :]
        s_ref[h] = states[h] * jnp.exp(g_end) + _dot(ks[h] * jnp.exp(g_end - g_cols[h]), v_news[h], _TN)
    for h in hs:
        o = os_[h]
        o = o * lax.rsqrt(jnp.mean(o * o, axis=-1, keepdims=True) + NORM_EPS) * nw_ref[...]
        z = z_ref[:, h * dim:(h + 1) * dim]
        o_ref[:, h * dim:(h + 1) * dim] = (o * (z * jax.nn.sigmoid(z))).astype(o_ref.dtype)

    @pl.when(c == pl.num_programs(1) - 1)
    def _():
        sfin_ref[...] = s_ref[...]


def _gdn_branch(qkv, ab, z, conv_buf, state, lp, t_valid, chunk):
    b, tp, w3 = qkv.shape
    heads = lp['gdn_a_log'].shape[0]
    dim = lp['gdn_norm_w'].shape[0]
    w = heads * dim
    taps = lp['gdn_conv_w'].shape[0]
    assert taps - 1 <= SUBLANE and t_valid >= taps - 1 and chunk <= LANE
    hist = jnp.pad(conv_buf.astype(F32), ((0, 0), (SUBLANE - (taps - 1), 0), (0, 0)))

    def lane_row(p):
        return jnp.pad(p.astype(F32), (0, LANE - p.shape[0])).reshape(1, LANE)

    consts = [lp['gdn_conv_w'].astype(F32), lane_row(lp['gdn_a_log']), lane_row(lp['gdn_dt_bias']),
              lp['gdn_norm_w'].reshape(1, dim).astype(F32)]
    const_specs = [pl.BlockSpec(a.shape, lambda bi, c: (0, 0)) for a in consts]
    state_spec = pl.BlockSpec((None, heads, dim, dim), lambda bi, c: (bi, 0, 0, 0))

    def rows(n):
        return pl.BlockSpec((None, chunk, n), lambda bi, c: (bi, c, 0))

    o, s_fin = pl.pallas_call(
        functools.partial(_gdn_kernel, heads=heads, dim=dim, chunk=chunk, t_valid=t_valid, taps=taps),
        grid=(b, tp // chunk),
        in_specs=[rows(w3), rows(2 * LANE), rows(w), pl.BlockSpec((None, SUBLANE, w3), lambda bi, c: (bi, 0, 0)),
                  state_spec] + const_specs,
        out_specs=[rows(w), state_spec],
        out_shape=[jax.ShapeDtypeStruct((b, tp, w), BF16), jax.ShapeDtypeStruct((b, heads, dim, dim), F32)],
        scratch_shapes=[pltpu.VMEM((chunk + SUBLANE, w3), F32), pltpu.VMEM((heads, dim, dim), F32)],
        compiler_params=_params("parallel", "arbitrary"),
        name="gated_deltanet",
    )(qkv, ab, z, hist, state.astype(F32), *consts)
    conv_new = qkv[:, t_valid - (taps - 1):t_valid]
    return o, conv_new, s_fin


def _peer_gate_kernel(qt_ref, khi_ref, klo_ref, g_ref, s_ref, sv_ref, cand_ref, fv_ref, e0_ref, e1_ref, gt_ref, *,
                      heads, n_keys, half):
    tt = qt_ref.shape[1]
    neg_inf = jnp.float32(-jnp.inf)
    key_iota = lax.broadcasted_iota(jnp.int32, (n_keys, tt), 0)

    together = 4
    for grp0 in range(0, 2 * heads, together):
        grps = range(grp0, grp0 + together)
        blks = []
        for grp in grps:
            p, h = divmod(grp, heads)
            q_hi, q_lo = _split_bf16(qt_ref[(2 * h + p) * half:(2 * h + p + 1) * half, :])
            s = (jnp.dot(khi_ref[grp], q_hi, preferred_element_type=F32)
                 + jnp.dot(khi_ref[grp], q_lo, preferred_element_type=F32)
                 + jnp.dot(klo_ref[grp], q_hi, preferred_element_type=F32))
            s_ref[grp * n_keys:(grp + 1) * n_keys, :] = s
            blks.append(s)

        def extract(n, blks, grps=grps):
            ms = [jnp.max(blk, axis=0, keepdims=True) for blk in blks]
            firsts = [jnp.min(jnp.where(blk == m, key_iota, n_keys), axis=0, keepdims=True)
                      for blk, m in zip(blks, ms)]
            for grp, m in zip(grps, ms):
                sv_ref[n, grp:grp + 1, :] = m
            return tuple(jnp.where(key_iota == first, neg_inf, blk) for blk, first in zip(blks, firsts))

        lax.fori_loop(0, PEER_TOPK, extract, tuple(blks))

    pairs = [(i, j) for i in range(PEER_TOPK) for j in range(PEER_TOPK) if (i + 1) * (j + 1) <= PEER_TOPK]
    for pos, (i, j) in enumerate(pairs):
        cand_ref[pos] = sv_ref[i, 0:heads, :] + sv_ref[j, heads:2 * heads, :]
    pair_iota = lax.broadcasted_iota(jnp.int32, (len(pairs), heads, tt), 0)

    def extract_pair(n, carry):
        c = cand_ref[...]
        m = jnp.max(c, axis=0)
        first = jnp.min(jnp.where(c == m[None], pair_iota, len(pairs)), axis=0)
        fv_ref[n] = m
        cand_ref[...] = jnp.where(pair_iota == first[None], neg_inf, c)
        return carry

    lax.fori_loop(0, PEER_TOPK, extract_pair, 0)
    fv = fv_ref[...]
    top, tau = fv[0], fv[PEER_TOPK - 1]
    inv_z = 1.0 / jnp.sum(jnp.exp(fv - top[None]), axis=0)

    m0, m1 = sv_ref[0, 0:heads, :], sv_ref[0, heads:2 * heads, :]
    for h in range(heads):
        s0 = s_ref[h * n_keys:(h + 1) * n_keys, :]
        s1 = s_ref[(heads + h) * n_keys:(heads + h + 1) * n_keys, :]
        e0_ref[h * n_keys:(h + 1) * n_keys, :] = jnp.exp(s0 - m0[h:h + 1, :]) * inv_z[h:h + 1, :]
        e1_ref[h * n_keys:(h + 1) * n_keys, :] = jnp.exp(s1 - m1[h:h + 1, :])

    def expert_rows(a, carry):
        acc = jnp.zeros((n_keys, tt), F32)
        for h in range(heads):
            s_ab = s_ref[pl.ds(h * n_keys + a, 1), :] + s_ref[(heads + h) * n_keys:(heads + h + 1) * n_keys, :]
            w_ab = e0_ref[pl.ds(h * n_keys + a, 1), :] * e1_ref[h * n_keys:(h + 1) * n_keys, :]
            acc = acc + jnp.where(s_ab >= tau[h:h + 1, :], w_ab, 0.0)
        gt_ref[pl.ds(pl.multiple_of(a * n_keys, n_keys), n_keys), :] = acc
        return carry

    lax.fori_loop(0, n_keys, expert_rows, 0)
    g_ref[...] = gt_ref[...].T


def _peer_gates(qt, subkeys):
    n_tok = qt.shape[1]
    heads, _, n_keys, half = subkeys.shape
    khi, klo = _split_bf16(jnp.transpose(subkeys.astype(F32), (1, 0, 2, 3)).reshape(2 * heads, n_keys, half))
    tt = 256 if n_tok % 256 == 0 else n_tok
    n_exp = n_keys * n_keys
    n_pairs = sum(PEER_TOPK // (i + 1) for i in range(PEER_TOPK))
    k_spec = pl.BlockSpec((2 * heads, n_keys, half), lambda i: (0, 0, 0))
    return pl.pallas_call(
        functools.partial(_peer_gate_kernel, heads=heads, n_keys=n_keys, half=half),
        grid=(n_tok // tt,),
        in_specs=[pl.BlockSpec((2 * heads * half, tt), lambda i: (0, i)), k_spec, k_spec],
        out_specs=pl.BlockSpec((tt, n_exp), lambda i: (i, 0)),
        out_shape=jax.ShapeDtypeStruct((n_tok, n_exp), F32),
        scratch_shapes=[pltpu.VMEM((2 * heads * n_keys, tt), F32), pltpu.VMEM((PEER_TOPK, 2 * heads, tt), F32),
                        pltpu.VMEM((n_pairs, heads, tt), F32), pltpu.VMEM((PEER_TOPK, heads, tt), F32),
                        pltpu.VMEM((heads * n_keys, tt), F32), pltpu.VMEM((heads * n_keys, tt), F32),
                        pltpu.VMEM((n_exp, tt), F32)],
        compiler_params=_params("parallel"),
        name="peer_gates",
    )(qt, khi, klo)


def _layer(x, lp, t_valid, sb_cache_k, sb_cache_v, rw_shift, rw_state, gdn_conv, gdn_state, prompt):
    b, tp, d = x.shape
    rows = b * tp
    x2 = x.reshape(rows, d)
    xn = _rmsnorm(x2, lp['norm_mix'], BF16)
    cols_sb = _mm(xn, lp['w_sb'])
    cols_rw = _mm(xn, lp['w_rw'])
    cols_gdn = _mm(xn, lp['w_gdn'])
    cols_ab = _mm(xn, lp['w_ab'])
    cols_z = _mm(xn, lp['w_z'])
    gates = _mm(xn, lp['w_gate'])

    sb_w = lp['w_sb'].shape[1] // 3
    q_sb = cols_sb[:, :sb_w].reshape(b, tp, sb_w)
    k_sb = cols_sb[:, sb_w:2 * sb_w].reshape(b, tp, sb_w)
    v_sb = cols_sb[:, 2 * sb_w:].reshape(b, tp, sb_w)
    if prompt:
        o_sb = _sb_attention(q_sb, k_sb.astype(BF16), v_sb.astype(BF16), q_off=0)
    else:
        past = sb_cache_k.shape[1]
        kpad = _round_up(past + tp, 128) - (past + tp)

        def keys(c, n):
            a = jnp.concatenate([c.reshape(b, past, sb_w), n], axis=1)
            return jnp.pad(a, ((0, 0), (0, kpad), (0, 0))).astype(BF16)

        o_sb = _sb_attention(q_sb, keys(sb_cache_k, k_sb), keys(sb_cache_v, v_sb), q_off=past)

    chunk = CHUNK if prompt else tp
    o_rw, rw_shift_new, rw_state_new = _rwkv_branch(cols_rw.reshape(b, tp, -1), rw_shift, rw_state, lp, t_valid, chunk)
    o_gdn, gdn_conv_new, gdn_state_new = _gdn_branch(
        cols_gdn.reshape(b, tp, -1), cols_ab.reshape(b, tp, -1), cols_z.reshape(b, tp, -1), gdn_conv, gdn_state, lp,
        t_valid, chunk)

    merged = _merge(o_sb.reshape(rows, sb_w), o_rw.reshape(rows, -1), o_gdn.reshape(rows, -1), lp['w_branch'], gates)
    x2 = _mm(merged, lp['w_out'], res=x2)

    xn2 = _rmsnorm(x2, lp['norm_ffn'], BF16)
    qt = _mm(lp['peer_wq_t'], xn2, nt=True)
    g_dense = _peer_gates(qt, lp['peer_subkeys'])
    gh = _peer_hid(xn2, lp['peer_u'], g_dense)
    x2 = _mm(gh, lp['peer_v'], res=x2)

    heads = sb_w // SB_DIM
    new = (k_sb[:, :t_valid].reshape(b, t_valid, heads, SB_DIM), v_sb[:, :t_valid].reshape(b, t_valid, heads, SB_DIM),
           rw_shift_new, rw_state_new, gdn_conv_new, gdn_state_new)
    return x2.reshape(b, tp, d), new


def kernel(x_prompt, x_sample, cache_sb_k, cache_sb_v, state_rw_shift, state_rw, state_gdn_conv, state_gdn, meta_tokens, norm_mix, w_in, rw_mu, rw_w0, rw_w2, rw_a0, rw_a2, rw_g2, rw_k_k, rw_k_a, rw_r_k, rw_ln_w, rw_ln_b, gdn_conv_w, gdn_a_log, gdn_dt_bias, gdn_norm_w, w_branch, w_out, norm_ffn, peer_wq, peer_subkeys, peer_u, peer_v, norm_final):
    depth = w_in.shape[0]
    b_p, seq, d = x_prompt.shape
    n_meta = meta_tokens.shape[0]
    sb_w = w_branch.shape[2]
    rw_cols = rw_mu.shape[1]
    gdn_heads, gdn_dim = gdn_a_log.shape[1], gdn_norm_w.shape[1]
    gdn_w = gdn_heads * gdn_dim
    rw_heads, rw_dim = rw_r_k.shape[1], rw_r_k.shape[2]
    gdn_taps = gdn_conv_w.shape[1]
    off_rw = 3 * sb_w
    off_gdn = off_rw + rw_cols
    off_ab = off_gdn + 3 * gdn_w
    off_z = off_ab + 2 * gdn_heads
    off_gate = off_z + gdn_w

    t_p = n_meta + seq
    tp_pad = _round_up(t_p, 256)
    hp = jnp.concatenate([jnp.broadcast_to(meta_tokens.astype(x_prompt.dtype)[None], (b_p, n_meta, d)), x_prompt],
                         axis=1)
    hp = jnp.pad(hp, ((0, 0), (0, tp_pad - t_p), (0, 0)))
    hs = x_sample
    t_s = x_sample.shape[1]

    p_rows, s_rows = [], []
    for l in range(depth):
        wl = w_in[l]
        lp = dict(
            norm_mix=norm_mix[l], norm_ffn=norm_ffn[l],
            rw_mu=rw_mu[l], rw_w0=rw_w0[l], rw_w2=rw_w2[l], rw_a0=rw_a0[l], rw_a2=rw_a2[l], rw_g2=rw_g2[l],
            rw_k_k=rw_k_k[l], rw_k_a=rw_k_a[l], rw_r_k=rw_r_k[l], rw_ln_w=rw_ln_w[l], rw_ln_b=rw_ln_b[l],
            gdn_conv_w=gdn_conv_w[l], gdn_a_log=gdn_a_log[l], gdn_dt_bias=gdn_dt_bias[l], gdn_norm_w=gdn_norm_w[l],
            w_branch=w_branch[l].astype(BF16), w_out=w_out[l].astype(BF16),
            peer_wq_t=peer_wq[l].T.astype(BF16), peer_subkeys=peer_subkeys[l],
            peer_u=peer_u[l].astype(BF16), peer_v=peer_v[l].astype(BF16))
        lp.update(
            w_sb=wl[:, :off_rw].astype(BF16),
            w_rw=_pad_segments(wl[:, off_rw:off_gdn], _rw_segments(lp)).astype(BF16),
            w_gdn=wl[:, off_gdn:off_ab].astype(BF16),
            w_ab=_pad_segments(wl[:, off_ab:off_z], [(gdn_heads, LANE)] * 2).astype(BF16),
            w_z=wl[:, off_z:off_gate].astype(BF16),
            w_gate=wl[:, off_gate:].astype(BF16))
        hp, p_new = _layer(
            hp, lp, t_p, None, None,
            jnp.zeros((b_p, 1, rw_cols), F32), jnp.zeros((b_p, rw_heads, rw_dim, rw_dim), F32),
            jnp.zeros((b_p, gdn_taps - 1, 3 * gdn_w), F32), jnp.zeros((b_p, gdn_heads, gdn_dim, gdn_dim), F32),
            True)
        hs, s_new = _layer(
            hs, lp, t_s, cache_sb_k[l], cache_sb_v[l], state_rw_shift[l], state_rw[l],
            state_gdn_conv[l], state_gdn[l], False)
        p_rows.append(p_new)
        s_rows.append(s_new)

    p_out = [jnp.stack(z) for z in zip(*p_rows)]
    s_out = [jnp.stack(z) for z in zip(*s_rows)]
    y_prompt = _rmsnorm(hp.reshape(b_p * tp_pad, d), norm_final, F32).reshape(b_p, tp_pad, d)[:, n_meta:t_p]
    b_s = hs.shape[0]
    y_sample = _rmsnorm(hs.reshape(b_s * t_s, d), norm_final, F32).reshape(b_s, t_s, d)
    return (y_prompt, y_sample, *p_out, *s_out)
```

```python
import functools

import jax
import jax.numpy as jnp
from jax import lax
from jax.experimental import pallas as pl
from jax.experimental.pallas import tpu as pltpu

F32 = jnp.float32
BF16 = jnp.bfloat16

V7X_VMEM_LIMIT_BYTES = 56 * 2**20
LANE = 128
SUBLANE = 8

NORM_EPS = 1e-6
RW_GN_EPS = 64e-5
CHUNK = 64
SB_DIM = 128
PEER_TOPK = 16
SB_LOG_UNDERFLOW = -104.0

_NT = (((1,), (1,)), ((), ()))
_TN = (((0,), (0,)), ((), ()))


def _round_up(n, m):
    return -(-n // m) * m


def _params(*sem):
    return pltpu.CompilerParams(dimension_semantics=sem, vmem_limit_bytes=V7X_VMEM_LIMIT_BYTES)


def _row_tile(rows):
    for t in (1280, 640, 256, 128, 64, 32, 16, 8):
        if rows % t == 0:
            return t
    raise ValueError(f"row count {rows} is not a multiple of 8")


def _col_tile(n):
    for t in (512, 384, 256, 128):
        if n % t == 0:
            return t
    raise ValueError(f"column count {n} is not a multiple of {LANE}")


def _split_bf16(x):
    hi = x.astype(BF16)
    return hi, (x - hi.astype(F32)).astype(BF16)


def _dot(a, b, dims=None):
    a, b = a.astype(BF16), b.astype(BF16)
    if dims is None:
        return jnp.dot(a, b, preferred_element_type=F32)
    return lax.dot_general(a, b, dims, preferred_element_type=F32)


def _dot_f32(a, b_hi, b_lo):
    a_hi, a_lo = _split_bf16(a)
    return (jnp.dot(a_hi, b_hi, preferred_element_type=F32) + jnp.dot(a_hi, b_lo, preferred_element_type=F32)
            + jnp.dot(a_lo, b_hi, preferred_element_type=F32))


def _dot_exact_lhs(a_bf16, b):
    b_hi, b_lo = _split_bf16(b)
    return jnp.dot(a_bf16, b_hi, preferred_element_type=F32) + jnp.dot(a_bf16, b_lo, preferred_element_type=F32)


def _dot_exact_rhs(a, b_bf16):
    a_hi, a_lo = _split_bf16(a)
    return jnp.dot(a_hi, b_bf16, preferred_element_type=F32) + jnp.dot(a_lo, b_bf16, preferred_element_type=F32)


def _softplus(x):
    return jnp.maximum(x, 0.0) + jnp.log1p(jnp.exp(-jnp.abs(x)))


def _tri_masks(n):
    r = lax.broadcasted_iota(jnp.int32, (n, n), 0)
    c = lax.broadcasted_iota(jnp.int32, (n, n), 1)
    return r >= c, r > c, r == c


def _chunk_incl_mask(rows, chunk):
    r = lax.broadcasted_iota(jnp.int32, (rows, rows), 0)
    c = lax.broadcasted_iota(jnp.int32, (rows, rows), 1)
    start = sum(jnp.where(r >= s * chunk, chunk, 0) for s in range(1, rows // chunk))
    return jnp.logical_and(r >= c, c >= start)


def _chunks_per_step(n_chunks):
    return 2 if n_chunks % 2 == 0 else 1


def _unit_lower_inverses(mats, n):
    _, _, eye = _tri_masks(n)
    xs = [jnp.where(eye, 1.0, 0.0) + a for a in mats]
    ps = list(mats)
    for _ in range(n.bit_length() - 2):
        ps = [_dot(p, p) for p in ps]
        xs = [x + _dot(x, p) for x, p in zip(xs, ps)]
    return xs


def _rmsnorm_kernel(x_ref, g_ref, o_ref):
    x = x_ref[...]
    ms = jnp.mean(x * x, axis=-1, keepdims=True)
    o_ref[...] = (x * lax.rsqrt(ms + NORM_EPS) * g_ref[...]).astype(o_ref.dtype)


def _rmsnorm(x, g, out_dtype):
    rows, d = x.shape
    tr = min(_row_tile(rows), 256)
    return pl.pallas_call(
        _rmsnorm_kernel,
        grid=(rows // tr,),
        in_specs=[pl.BlockSpec((tr, d), lambda i: (i, 0)), pl.BlockSpec((1, d), lambda i: (0, 0))],
        out_specs=pl.BlockSpec((tr, d), lambda i: (i, 0)),
        out_shape=jax.ShapeDtypeStruct((rows, d), out_dtype),
        compiler_params=_params("parallel"),
        name="rmsnorm",
    )(x, g.reshape(1, d).astype(F32))


def _mm_kernel(a_ref, b_ref, *rest, nt, has_res):
    o_ref = rest[-1]
    acc = lax.dot_general(a_ref[...], b_ref[...], _NT if nt else (((1,), (0,)), ((), ())),
                          preferred_element_type=F32)
    if has_res:
        acc = acc + rest[0][...]
    o_ref[...] = acc.astype(o_ref.dtype)


def _mm(a, b, *, nt=False, res=None, out_dtype=F32):
    m, k = a.shape
    n = b.shape[0] if nt else b.shape[1]
    tm, tn = (_col_tile(m), _row_tile(n)) if nt else (_row_tile(m), _col_tile(n))
    in_specs = [
        pl.BlockSpec((tm, k), lambda i, j: (i, 0)),
        pl.BlockSpec((tn, k), lambda i, j: (j, 0)) if nt else pl.BlockSpec((k, tn), lambda i, j: (0, j)),
    ]
    args = [a, b]
    if res is not None:
        in_specs.append(pl.BlockSpec((tm, tn), lambda i, j: (i, j)))
        args.append(res)
    return pl.pallas_call(
        functools.partial(_mm_kernel, nt=nt, has_res=res is not None),
        grid=(m // tm, n // tn),
        in_specs=in_specs,
        out_specs=pl.BlockSpec((tm, tn), lambda i, j: (i, j)),
        out_shape=jax.ShapeDtypeStruct((m, n), out_dtype),
        compiler_params=_params("parallel", "arbitrary"),
        name="matmul",
    )(*args)


def _merge_kernel(o0_ref, o1_ref, o2_ref, wb_ref, g0_ref, g1_ref, g2_ref, out_ref):
    acc = None
    for o_ref, g_ref, i in ((o0_ref, g0_ref, 0), (o1_ref, g1_ref, 1), (o2_ref, g2_ref, 2)):
        term = jax.nn.sigmoid(g_ref[...]) * jnp.dot(o_ref[...], wb_ref[i], preferred_element_type=F32)
        acc = term if acc is None else acc + term
    out_ref[...] = acc.astype(out_ref.dtype)


def _merge(o_sb, o_rw, o_gdn, wb, gates):
    m, w = o_sb.shape
    d = wb.shape[2]
    tm, tn = _row_tile(m), _col_tile(d)
    nj = d // tn
    o_spec = pl.BlockSpec((tm, w), lambda i, j: (i, 0))

    def g_spec(b):
        return pl.BlockSpec((tm, tn), lambda i, j: (i, j + b * nj))

    return pl.pallas_call(
        _merge_kernel,
        grid=(m // tm, nj),
        in_specs=[o_spec, o_spec, o_spec, pl.BlockSpec((3, w, tn), lambda i, j: (0, 0, j)),
                  g_spec(0), g_spec(1), g_spec(2)],
        out_specs=pl.BlockSpec((tm, tn), lambda i, j: (i, j)),
        out_shape=jax.ShapeDtypeStruct((m, d), BF16),
        compiler_params=_params("parallel", "arbitrary"),
        name="branch_merge",
    )(o_sb, o_rw, o_gdn, wb, gates, gates, gates)


def _peer_hid_kernel(x_ref, u_ref, g_ref, o_ref):
    h = lax.dot_general(x_ref[...], u_ref[...], _NT, preferred_element_type=F32)
    hid = 0.5 * h * (1.0 + lax.erf(h * (2.0 ** -0.5)))
    o_ref[...] = (g_ref[...] * hid).astype(o_ref.dtype)


def _peer_hid(xn, u, g_dense):
    m, d = xn.shape
    e = u.shape[0]
    tm, te = _row_tile(m), _col_tile(e)
    return pl.pallas_call(
        _peer_hid_kernel,
        grid=(m // tm, e // te),
        in_specs=[pl.BlockSpec((tm, d), lambda i, j: (i, 0)), pl.BlockSpec((te, d), lambda i, j: (j, 0)),
                  pl.BlockSpec((tm, te), lambda i, j: (i, j))],
        out_specs=pl.BlockSpec((tm, te), lambda i, j: (i, j)),
        out_shape=jax.ShapeDtypeStruct((m, e), BF16),
        compiler_params=_params("parallel", "arbitrary"),
        name="peer_hidden",
    )(xn, u, g_dense)


def _sb_attn_kernel(q_ref, k_ref, v_ref, o_ref, *, bq, bk, q_off, group):
    i = pl.program_id(2)
    qpos0 = q_off + i * bq
    hs = range(group)
    sls = [slice(h * SB_DIM, (h + 1) * SB_DIM) for h in hs]
    qs = [(q_ref[:, sl] * (SB_DIM ** -0.5)).astype(BF16) for sl in sls]
    qpos = qpos0 + lax.broadcasted_iota(jnp.int32, (bq, 1), 0)
    incl, _, _ = _tri_masks(bk)
    tri = incl.astype(BF16)
    kb_hi = jnp.maximum(qpos0 + bq - 2, 0) // bk

    def cond(c):
        kb, later, _ = c
        return jnp.logical_and(kb >= 0, jnp.max(functools.reduce(jnp.maximum, later)) > SB_LOG_UNDERFLOW)

    def body(c):
        kb, later, acc = c
        off = pl.multiple_of(kb * bk, bk)
        mask = (off + lax.broadcasted_iota(jnp.int32, (1, bk), 1)) < qpos
        zs = [lax.dot_general(qs[h], k_ref[pl.ds(off, bk), sls[h]], _NT, preferred_element_type=F32) for h in hs]
        lfs = [jnp.where(mask, -_softplus(z), 0.0) for z in zs]
        revs = [_dot_exact_rhs(lf, tri) for lf in lfs]
        atts = [jnp.where(mask, jnp.exp(zs[h] + revs[h] + later[h]), 0.0).astype(BF16) for h in hs]
        acc = tuple(acc[h] + jnp.dot(atts[h], v_ref[pl.ds(off, bk), sls[h]], preferred_element_type=F32) for h in hs)
        return kb - 1, tuple(later[h] + revs[h][:, 0:1] for h in hs), acc

    zeros = lambda n: tuple(jnp.zeros((bq, n), F32) for _ in hs)
    _, _, acc = lax.while_loop(cond, body, (kb_hi, zeros(1), zeros(SB_DIM)))
    for h in hs:
        o_ref[:, sls[h]] = acc[h].astype(o_ref.dtype)


def _sb_attention(q, k, v, *, q_off):
    b, tq, w = q.shape
    tk = k.shape[1]
    heads = w // SB_DIM
    group = next(g for g in (4, 2, 1) if heads % g == 0)
    bq = min(_row_tile(tq), 256)
    bk = 128
    assert tk % bk == 0 and q_off + tq <= tk + 1
    kv_spec = pl.BlockSpec((None, tk, group * SB_DIM), lambda bi, h, i: (bi, 0, h), pipeline_mode=pl.Buffered(1))
    q_spec = pl.BlockSpec((None, bq, group * SB_DIM), lambda bi, h, i: (bi, i, h))
    return pl.pallas_call(
        functools.partial(_sb_attn_kernel, bq=bq, bk=bk, q_off=q_off, group=group),
        grid=(b, heads // group, tq // bq),
        in_specs=[q_spec, kv_spec, kv_spec],
        out_specs=q_spec,
        out_shape=jax.ShapeDtypeStruct((b, tq, w), BF16),
        compiler_params=_params("parallel", "parallel", "arbitrary"),
        name="sb_attention",
    )(q, k, v)


def _rwkv_kernel(cols_ref, hist_ref, h0_ref, mu_ref, w0_ref, w2_ref, a0_ref, a2_ref, g2_ref, kk_ref, ka_ref, rk_ref,
                 lnw_ref, lnb_ref, gred_ref, gexp_ref, o_ref, hfin_ref, ext_ref, h_ref, oacc_ref, *,
                 heads, dim, chunk, sub, t_valid, lora_pad):
    c = pl.program_id(1)
    w = heads * dim
    rows = chunk * sub
    n_dec, n_aaa, n_gate = lora_pad

    @pl.when(c == 0)
    def _():
        ext_ref[0:SUBLANE, :] = hist_ref[...]
        h_ref[...] = h0_ref[...]

    x = cols_ref[...]
    ext_ref[SUBLANE:SUBLANE + rows, :] = x
    prev = ext_ref[SUBLANE - 1:SUBLANE - 1 + rows, :]
    ext_ref[0:SUBLANE, :] = x[rows - SUBLANE:rows, :]
    xs = x + (prev - x) * mu_ref[...]
    r, k, v = xs[:, :w], xs[:, w:2 * w], xs[:, 2 * w:3 * w]
    xw = xs[:, 3 * w:3 * w + n_dec]
    xa = xs[:, 3 * w + n_dec:3 * w + n_dec + n_aaa]
    xg = xs[:, 3 * w + n_dec + n_aaa:3 * w + n_dec + n_aaa + n_gate]
    log_w = -_softplus(-(w0_ref[...] + _dot_f32(jnp.tanh(xw), w2_ref[0], w2_ref[1]))) - 0.5
    logd = -jnp.exp(log_w)
    a = jax.nn.sigmoid(a0_ref[...] + _dot_f32(xa, a2_ref[0], a2_ref[1]))
    gate = _dot_f32(jax.nn.sigmoid(xg), g2_ref[0], g2_ref[1])

    def head_sums(t):
        return _dot_exact_rhs(_dot_exact_rhs(t, gred_ref[...]), gexp_ref[...])

    kk = k * kk_ref[...]
    k2 = k * (1.0 + (a - 1.0) * ka_ref[...])
    sums = head_sums(jnp.concatenate([kk * kk, r * k2 * rk_ref[...]], axis=0))
    kk = kk / jnp.maximum(jnp.sqrt(sums[:rows]), 1e-12)
    bonus = sums[rows:] * v

    valid = (c * rows + lax.broadcasted_iota(jnp.int32, (rows, 1), 0)) < t_valid
    logd = jnp.where(valid, logd, 0.0)
    k_m = jnp.where(valid, k2, 0.0)
    v_m = jnp.where(valid, v, 0.0)
    a_m = jnp.where(valid, -kk, 0.0)
    b_m = jnp.where(valid, kk * a, 0.0)

    g_in = _dot_exact_lhs(_chunk_incl_mask(rows, chunk).astype(BF16), logd)
    r_t = r * jnp.exp(g_in)
    a_t = a_m * jnp.exp(g_in - logd)
    e_neg = jnp.exp(-g_in)
    b_h, k_h = b_m * e_neg, k_m * e_neg
    _, strict, _ = _tri_masks(chunk)
    row2 = lax.broadcasted_iota(jnp.int32, (chunk, 2 * chunk), 0)
    col2 = lax.broadcasted_iota(jnp.int32, (chunk, 2 * chunk), 1)
    incl2 = row2 >= jnp.where(col2 >= chunk, col2 - chunk, col2)

    hs = range(heads)
    items = [(slice(s * chunk, (s + 1) * chunk), slice(h * dim, (h + 1) * dim)) for s in range(sub) for h in hs]
    a_hs = [a_t[rs, sl] for rs, sl in items]
    v_hs = [v_m[rs, sl] for rs, sl in items]
    r_hs = [r_t[rs, sl] for rs, sl in items]
    bk_hs = [jnp.concatenate([b_h[rs, sl], k_h[rs, sl]], axis=0) for rs, sl in items]
    a_ab = [jnp.where(strict, _dot(a_i, b_h[rs, sl], _NT), 0.0) for a_i, (rs, sl) in zip(a_hs, items)]
    a_ak = [jnp.where(strict, _dot(a_i, k_h[rs, sl], _NT), 0.0) for a_i, (rs, sl) in zip(a_hs, items)]
    a_rbk = [jnp.where(incl2, _dot(r_i, bk_i, _NT), 0.0) for r_i, bk_i in zip(r_hs, bk_hs)]
    akv = [_dot(ak_i, v_i) for ak_i, v_i in zip(a_ak, v_hs)]
    t_inv = _unit_lower_inverses(a_ab, chunk)
    w_mat = [_dot(t_i, a_i) for t_i, a_i in zip(t_inv, a_hs)]
    u0 = [_dot(t_i, akv_i) for t_i, akv_i in zip(t_inv, akv)]

    state = [h_ref[h] for h in hs]
    for s in range(sub):
        rs = slice(s * chunk, (s + 1) * chunk)
        g_last = g_in[(s + 1) * chunk - 1:(s + 1) * chunk, :]
        e_end = jnp.exp(g_last - g_in[rs])
        b_end, k_end = b_m[rs] * e_end, k_m[rs] * e_end
        g_tot = jnp.exp(g_last)
        idx = [s * heads + h for h in hs]
        wr = [_dot(jnp.concatenate([w_mat[i], r_hs[i]], axis=0), state[h]) for h, i in zip(hs, idx)]
        uv = [jnp.concatenate([u0[i] + wr[h][:chunk], v_hs[i]], axis=0) for h, i in zip(hs, idx)]
        for h, i in zip(hs, idx):
            oacc_ref[rs, items[i][1]] = wr[h][chunk:] + _dot(a_rbk[i], uv[h])
        new_state = []
        for h, i in zip(hs, idx):
            sl = items[i][1]
            end_h = jnp.concatenate([b_end[:, sl], k_end[:, sl]], axis=0)
            new_state.append(state[h] * _row_to_col(g_tot[:, sl], dim) + _dot(end_h, uv[h], _TN))
        state = new_state
    for h in hs:
        h_ref[h] = state[h]

    o = oacc_ref[...]
    cen = o - head_sums(o) * (1.0 / dim)
    var = head_sums(cen * cen) * (1.0 / dim)
    gn = cen * lax.rsqrt(var + RW_GN_EPS) * lnw_ref[...] + lnb_ref[...]
    o_ref[...] = ((gn + bonus) * gate).astype(o_ref.dtype)

    @pl.when(c == pl.num_programs(1) - 1)
    def _():
        hfin_ref[...] = h_ref[...]


def _row_to_col(row, n):
    _, _, eye = _tri_masks(n)
    return jnp.sum(jnp.where(eye, row, 0.0), axis=1, keepdims=True)


def _rw_segments(lp):
    w = lp['rw_w0'].shape[0]
    loras = [lp['rw_w2'].shape[0], lp['rw_a2'].shape[0], lp['rw_g2'].shape[0]]
    return [(w, w)] * 3 + [(n, _round_up(n, LANE)) for n in loras]


def _pad_segments(x, segs):
    out, off = [], 0
    for n, n_pad in segs:
        piece = x[..., off:off + n]
        out.append(jnp.pad(piece, [(0, 0)] * (x.ndim - 1) + [(0, n_pad - n)]))
        off += n
    return jnp.concatenate(out, axis=-1)


def _unpad_segments(x, segs):
    out, off = [], 0
    for n, n_pad in segs:
        out.append(x[..., off:off + n])
        off += n_pad
    return jnp.concatenate(out, axis=-1)


def _pad_rows(w, n):
    return jnp.pad(w, ((0, n - w.shape[0]), (0, 0)))


def _rwkv_branch(cols, shift_buf, state, lp, t_valid, chunk):
    b, tp, n_pad = cols.shape
    heads, dim = lp['rw_r_k'].shape
    w = heads * dim
    segs = _rw_segments(lp)
    lora_pad = tuple(p for _, p in segs[3:])
    hist = jnp.pad(_pad_segments(shift_buf.astype(F32), segs), ((0, 0), (SUBLANE - 1, 0), (0, 0)))
    assert heads <= LANE
    gred = (jnp.arange(w)[:, None] // dim == jnp.arange(LANE)[None, :]).astype(BF16)
    gexp = gred.T

    def row(p):
        return p.reshape(1, -1).astype(F32)

    def lora(p, n):
        return jnp.stack(_split_bf16(_pad_rows(p.astype(F32), n)))

    consts = [row(_pad_segments(lp['rw_mu'], segs)), row(lp['rw_w0']), lora(lp['rw_w2'], lora_pad[0]),
              row(lp['rw_a0']), lora(lp['rw_a2'], lora_pad[1]), lora(lp['rw_g2'], lora_pad[2]),
              row(lp['rw_k_k']), row(lp['rw_k_a']), row(lp['rw_r_k']), row(lp['rw_ln_w']), row(lp['rw_ln_b']),
              gred, gexp]
    const_specs = [pl.BlockSpec(a.shape, lambda bi, c, nd=a.ndim: (0,) * nd) for a in consts]
    state_spec = pl.BlockSpec((None, heads, dim, dim), lambda bi, c: (bi, 0, 0, 0))
    sub = _chunks_per_step(tp // chunk)
    rows = chunk * sub
    o, h_fin = pl.pallas_call(
        functools.partial(_rwkv_kernel, heads=heads, dim=dim, chunk=chunk, sub=sub, t_valid=t_valid,
                          lora_pad=lora_pad),
        grid=(b, tp // rows),
        in_specs=[pl.BlockSpec((None, rows, n_pad), lambda bi, c: (bi, c, 0)),
                  pl.BlockSpec((None, SUBLANE, n_pad), lambda bi, c: (bi, 0, 0)), state_spec] + const_specs,
        out_specs=[pl.BlockSpec((None, rows, w), lambda bi, c: (bi, c, 0)), state_spec],
        out_shape=[jax.ShapeDtypeStruct((b, tp, w), BF16), jax.ShapeDtypeStruct((b, heads, dim, dim), F32)],
        scratch_shapes=[pltpu.VMEM((rows + SUBLANE, n_pad), F32), pltpu.VMEM((heads, dim, dim), F32),
                        pltpu.VMEM((rows, w), F32)],
        compiler_params=_params("parallel", "arbitrary"),
        name="rwkv7",
    )(cols, hist, state.astype(F32), *consts)
    shift_new = _unpad_segments(cols[:, t_valid - 1:t_valid], segs)
    return o, shift_new, h_fin


def _gdn_kernel(qkv_ref, ab_ref, z_ref, hist_ref, s0_ref, cw_ref, alog_ref, dtb_ref, nw_ref, o_ref, sfin_ref,
                ext_ref, s_ref, *, heads, dim, chunk, sub, t_valid, taps):
    c = pl.program_id(1)
    w = heads * dim
    rows = chunk * sub

    @pl.when(c == 0)
    def _():
        ext_ref[0:SUBLANE, :] = hist_ref[...]
        s_ref[...] = s0_ref[...]

    x = qkv_ref[...]
    ext_ref[SUBLANE:SUBLANE + rows, :] = x
    first = SUBLANE - (taps - 1)
    conv = ext_ref[first:first + rows, :] * cw_ref[0:1, :]
    for i in range(1, taps):
        conv = conv + ext_ref[first + i:first + i + rows, :] * cw_ref[i:i + 1, :]
    ext_ref[0:SUBLANE, :] = x[rows - SUBLANE:rows, :]
    act = conv * jax.nn.sigmoid(conv)

    valid = (c * rows + lax.broadcasted_iota(jnp.int32, (rows, 1), 0)) < t_valid
    ab = ab_ref[...]
    beta = jnp.where(valid, jax.nn.sigmoid(ab[:, :LANE]), 0.0)
    g = jnp.where(valid, -jnp.exp(alog_ref[...]) * _softplus(ab[:, LANE:] + dtb_ref[...]), 0.0)
    incl, strict, _ = _tri_masks(chunk)
    gc = _dot_exact_lhs(_chunk_incl_mask(rows, chunk).astype(BF16), g)
    gc_sq = gc if rows == LANE else jnp.concatenate([gc, jnp.zeros((LANE - rows, LANE), F32)], axis=0)
    gc_t = gc_sq.T

    hs = range(heads)
    qs, ks, vs, kbs, ms, attns, e_gcs, g_cols = [], [], [], [], [], [], [], []
    for s in range(sub):
        rs = slice(s * chunk, (s + 1) * chunk)
        ok = valid[rs]
        for h in hs:
            q = jnp.where(ok, act[rs, h * dim:(h + 1) * dim], 0.0)
            k = jnp.where(ok, act[rs, w + h * dim:w + (h + 1) * dim], 0.0)
            v = jnp.where(ok, act[rs, 2 * w + h * dim:2 * w + (h + 1) * dim], 0.0)
            q = q * lax.rsqrt(jnp.sum(q * q, axis=-1, keepdims=True) + 1e-6) * (dim ** -0.5)
            k = k * lax.rsqrt(jnp.sum(k * k, axis=-1, keepdims=True) + 1e-6)
            b_h = beta[rs, h:h + 1]
            g_col = gc[rs, h:h + 1]
            decay = jnp.exp(jnp.minimum(g_col - gc_t[h:h + 1, rs], 0.0))
            kb = k * b_h
            qk = _dot(jnp.concatenate([kb, q], axis=0), k, _NT)
            ms.append(jnp.where(strict, qk[:chunk] * decay, 0.0))
            attns.append(jnp.where(incl, qk[chunk:] * decay, 0.0))
            qs.append(q), ks.append(k), vs.append(v * b_h), kbs.append(kb)
            e_gcs.append(jnp.exp(g_col)), g_cols.append(g_col)
    t_inv = _unit_lower_inverses([-m for m in ms], chunk)
    us = [_dot(t_i, v_i) for t_i, v_i in zip(t_inv, vs)]
    w_mats = [_dot(t_i, kb_i * e_i) for t_i, kb_i, e_i in zip(t_inv, kbs, e_gcs)]

    states = [s_ref[h] for h in hs]
    for s in range(sub):
        rs = slice(s * chunk, (s + 1) * chunk)
        idx = [s * heads + h for h in hs]
        wss = [_dot(jnp.concatenate([w_mats[i], qs[i] * e_gcs[i]], axis=0), states[h]) for h, i in zip(hs, idx)]
        v_news = [us[i] - wss[h][:chunk] for h, i in zip(hs, idx)]
        os_ = [wss[h][chunk:] + _dot(attns[i], v_news[h]) for h, i in zip(hs, idx)]
        new_states = []
        for h, i in zip(hs, idx):
            g_end = g_cols[i][chunk - 1:chunk, :]
            new_states.append(states[h] * jnp.exp(g_end) + _dot(ks[i] * jnp.exp(g_end - g_cols[i]), v_news[h], _TN))
        states = new_states
        for h in hs:
            o = os_[h]
            o = o * lax.rsqrt(jnp.mean(o * o, axis=-1, keepdims=True) + NORM_EPS) * nw_ref[...]
            z = z_ref[rs, h * dim:(h + 1) * dim]
            o_ref[rs, h * dim:(h + 1) * dim] = (o * (z * jax.nn.sigmoid(z))).astype(o_ref.dtype)
    for h in hs:
        s_ref[h] = states[h]

    @pl.when(c == pl.num_programs(1) - 1)
    def _():
        sfin_ref[...] = s_ref[...]


def _gdn_branch(qkv, ab, z, conv_buf, state, lp, t_valid, chunk):
    b, tp, w3 = qkv.shape
    heads = lp['gdn_a_log'].shape[0]
    dim = lp['gdn_norm_w'].shape[0]
    w = heads * dim
    taps = lp['gdn_conv_w'].shape[0]
    assert taps - 1 <= SUBLANE and t_valid >= taps - 1 and chunk <= LANE
    hist = jnp.pad(conv_buf.astype(F32), ((0, 0), (SUBLANE - (taps - 1), 0), (0, 0)))

    def lane_row(p):
        return jnp.pad(p.astype(F32), (0, LANE - p.shape[0])).reshape(1, LANE)

    consts = [lp['gdn_conv_w'].astype(F32), lane_row(lp['gdn_a_log']), lane_row(lp['gdn_dt_bias']),
              lp['gdn_norm_w'].reshape(1, dim).astype(F32)]
    const_specs = [pl.BlockSpec(a.shape, lambda bi, c: (0, 0)) for a in consts]
    state_spec = pl.BlockSpec((None, heads, dim, dim), lambda bi, c: (bi, 0, 0, 0))

    sub = _chunks_per_step(tp // chunk)
    step_rows = chunk * sub
    assert step_rows <= LANE

    def rows(n):
        return pl.BlockSpec((None, step_rows, n), lambda bi, c: (bi, c, 0))

    o, s_fin = pl.pallas_call(
        functools.partial(_gdn_kernel, heads=heads, dim=dim, chunk=chunk, sub=sub, t_valid=t_valid, taps=taps),
        grid=(b, tp // step_rows),
        in_specs=[rows(w3), rows(2 * LANE), rows(w), pl.BlockSpec((None, SUBLANE, w3), lambda bi, c: (bi, 0, 0)),
                  state_spec] + const_specs,
        out_specs=[rows(w), state_spec],
        out_shape=[jax.ShapeDtypeStruct((b, tp, w), BF16), jax.ShapeDtypeStruct((b, heads, dim, dim), F32)],
        scratch_shapes=[pltpu.VMEM((step_rows + SUBLANE, w3), F32), pltpu.VMEM((heads, dim, dim), F32)],
        compiler_params=_params("parallel", "arbitrary"),
        name="gated_deltanet",
    )(qkv, ab, z, hist, state.astype(F32), *consts)
    conv_new = qkv[:, t_valid - (taps - 1):t_valid]
    return o, conv_new, s_fin


def _peer_gate_kernel(qt_ref, khi_ref, klo_ref, g_ref, s_ref, sv_ref, cand_ref, fv_ref, e0_ref, e1_ref, gt_ref, *,
                      heads, n_keys, half):
    tt = qt_ref.shape[1]
    neg_inf = jnp.float32(-jnp.inf)
    key_iota = lax.broadcasted_iota(jnp.int32, (n_keys, tt), 0)

    together = 4
    for grp0 in range(0, 2 * heads, together):
        grps = range(grp0, grp0 + together)
        blks = []
        for grp in grps:
            p, h = divmod(grp, heads)
            q_hi, q_lo = _split_bf16(qt_ref[(2 * h + p) * half:(2 * h + p + 1) * half, :])
            s = (jnp.dot(khi_ref[grp], q_hi, preferred_element_type=F32)
                 + jnp.dot(khi_ref[grp], q_lo, preferred_element_type=F32)
                 + jnp.dot(klo_ref[grp], q_hi, preferred_element_type=F32))
            s_ref[grp * n_keys:(grp + 1) * n_keys, :] = s
            blks.append(s)

        def extract(n, blks, grps=grps):
            ms = [jnp.max(blk, axis=0, keepdims=True) for blk in blks]
            firsts = [jnp.min(jnp.where(blk == m, key_iota, n_keys), axis=0, keepdims=True)
                      for blk, m in zip(blks, ms)]
            for grp, m in zip(grps, ms):
                sv_ref[n, grp:grp + 1, :] = m
            return tuple(jnp.where(key_iota == first, neg_inf, blk) for blk, first in zip(blks, firsts))

        lax.fori_loop(0, PEER_TOPK, extract, tuple(blks))

    pairs = [(i, j) for i in range(PEER_TOPK) for j in range(PEER_TOPK) if (i + 1) * (j + 1) <= PEER_TOPK]
    for pos, (i, j) in enumerate(pairs):
        cand_ref[pos] = sv_ref[i, 0:heads, :] + sv_ref[j, heads:2 * heads, :]
    pair_iota = lax.broadcasted_iota(jnp.int32, (len(pairs), heads, tt), 0)

    def extract_pair(n, carry):
        c = cand_ref[...]
        m = jnp.max(c, axis=0)
        first = jnp.min(jnp.where(c == m[None], pair_iota, len(pairs)), axis=0)
        fv_ref[n] = m
        cand_ref[...] = jnp.where(pair_iota == first[None], neg_inf, c)
        return carry

    lax.fori_loop(0, PEER_TOPK, extract_pair, 0)
    fv = fv_ref[...]
    top, tau = fv[0], fv[PEER_TOPK - 1]
    inv_z = 1.0 / jnp.sum(jnp.exp(fv - top[None]), axis=0)

    m0, m1 = sv_ref[0, 0:heads, :], sv_ref[0, heads:2 * heads, :]
    for h in range(heads):
        s0 = s_ref[h * n_keys:(h + 1) * n_keys, :]
        s1 = s_ref[(heads + h) * n_keys:(heads + h + 1) * n_keys, :]
        e0_ref[h * n_keys:(h + 1) * n_keys, :] = jnp.exp(s0 - m0[h:h + 1, :]) * inv_z[h:h + 1, :]
        e1_ref[h * n_keys:(h + 1) * n_keys, :] = jnp.exp(s1 - m1[h:h + 1, :])

    def expert_rows(a, carry):
        acc = jnp.zeros((n_keys, tt), F32)
        for h in range(heads):
            s_ab = s_ref[pl.ds(h * n_keys + a, 1), :] + s_ref[(heads + h) * n_keys:(heads + h + 1) * n_keys, :]
            w_ab = e0_ref[pl.ds(h * n_keys + a, 1), :] * e1_ref[h * n_keys:(h + 1) * n_keys, :]
            acc = acc + jnp.where(s_ab >= tau[h:h + 1, :], w_ab, 0.0)
        gt_ref[pl.ds(pl.multiple_of(a * n_keys, n_keys), n_keys), :] = acc
        return carry

    lax.fori_loop(0, n_keys, expert_rows, 0)
    g_ref[...] = gt_ref[...].T


def _peer_gates(qt, subkeys):
    n_tok = qt.shape[1]
    heads, _, n_keys, half = subkeys.shape
    khi, klo = _split_bf16(jnp.transpose(subkeys.astype(F32), (1, 0, 2, 3)).reshape(2 * heads, n_keys, half))
    tt = 256 if n_tok % 256 == 0 else n_tok
    n_exp = n_keys * n_keys
    n_pairs = sum(PEER_TOPK // (i + 1) for i in range(PEER_TOPK))
    k_spec = pl.BlockSpec((2 * heads, n_keys, half), lambda i: (0, 0, 0))
    return pl.pallas_call(
        functools.partial(_peer_gate_kernel, heads=heads, n_keys=n_keys, half=half),
        grid=(n_tok // tt,),
        in_specs=[pl.BlockSpec((2 * heads * half, tt), lambda i: (0, i)), k_spec, k_spec],
        out_specs=pl.BlockSpec((tt, n_exp), lambda i: (i, 0)),
        out_shape=jax.ShapeDtypeStruct((n_tok, n_exp), F32),
        scratch_shapes=[pltpu.VMEM((2 * heads * n_keys, tt), F32), pltpu.VMEM((PEER_TOPK, 2 * heads, tt), F32),
                        pltpu.VMEM((n_pairs, heads, tt), F32), pltpu.VMEM((PEER_TOPK, heads, tt), F32),
                        pltpu.VMEM((heads * n_keys, tt), F32), pltpu.VMEM((heads * n_keys, tt), F32),
                        pltpu.VMEM((n_exp, tt), F32)],
        compiler_params=_params("parallel"),
        name="peer_gates",
    )(qt, khi, klo)


def _layer(x, lp, t_valid, sb_cache_k, sb_cache_v, rw_shift, rw_state, gdn_conv, gdn_state, prompt):
    b, tp, d = x.shape
    rows = b * tp
    x2 = x.reshape(rows, d)
    xn = _rmsnorm(x2, lp['norm_mix'], BF16)
    cols_sb = _mm(xn, lp['w_sb'])
    cols_rw = _mm(xn, lp['w_rw'])
    cols_gdn = _mm(xn, lp['w_gdn'])
    cols_ab = _mm(xn, lp['w_ab'])
    cols_z = _mm(xn, lp['w_z'])
    gates = _mm(xn, lp['w_gate'])

    sb_w = lp['w_sb'].shape[1] // 3
    q_sb = cols_sb[:, :sb_w].reshape(b, tp, sb_w)
    k_sb = cols_sb[:, sb_w:2 * sb_w].reshape(b, tp, sb_w)
    v_sb = cols_sb[:, 2 * sb_w:].reshape(b, tp, sb_w)
    if prompt:
        o_sb = _sb_attention(q_sb, k_sb.astype(BF16), v_sb.astype(BF16), q_off=0)
    else:
        past = sb_cache_k.shape[1]
        kpad = _round_up(past + tp, 128) - (past + tp)

        def keys(c, n):
            a = jnp.concatenate([c.reshape(b, past, sb_w), n], axis=1)
            return jnp.pad(a, ((0, 0), (0, kpad), (0, 0))).astype(BF16)

        o_sb = _sb_attention(q_sb, keys(sb_cache_k, k_sb), keys(sb_cache_v, v_sb), q_off=past)

    chunk = CHUNK if prompt else tp
    o_rw, rw_shift_new, rw_state_new = _rwkv_branch(cols_rw.reshape(b, tp, -1), rw_shift, rw_state, lp, t_valid, chunk)
    o_gdn, gdn_conv_new, gdn_state_new = _gdn_branch(
        cols_gdn.reshape(b, tp, -1), cols_ab.reshape(b, tp, -1), cols_z.reshape(b, tp, -1), gdn_conv, gdn_state, lp,
        t_valid, chunk)

    merged = _merge(o_sb.reshape(rows, sb_w), o_rw.reshape(rows, -1), o_gdn.reshape(rows, -1), lp['w_branch'], gates)
    x2 = _mm(merged, lp['w_out'], res=x2)

    xn2 = _rmsnorm(x2, lp['norm_ffn'], BF16)
    qt = _mm(lp['peer_wq_t'], xn2, nt=True)
    g_dense = _peer_gates(qt, lp['peer_subkeys'])
    gh = _peer_hid(xn2, lp['peer_u'], g_dense)
    x2 = _mm(gh, lp['peer_v'], res=x2)

    heads = sb_w // SB_DIM
    new = (k_sb[:, :t_valid].reshape(b, t_valid, heads, SB_DIM), v_sb[:, :t_valid].reshape(b, t_valid, heads, SB_DIM),
           rw_shift_new, rw_state_new, gdn_conv_new, gdn_state_new)
    return x2.reshape(b, tp, d), new


def kernel(x_prompt, x_sample, cache_sb_k, cache_sb_v, state_rw_shift, state_rw, state_gdn_conv, state_gdn, meta_tokens, norm_mix, w_in, rw_mu, rw_w0, rw_w2, rw_a0, rw_a2, rw_g2, rw_k_k, rw_k_a, rw_r_k, rw_ln_w, rw_ln_b, gdn_conv_w, gdn_a_log, gdn_dt_bias, gdn_norm_w, w_branch, w_out, norm_ffn, peer_wq, peer_subkeys, peer_u, peer_v, norm_final):
    depth = w_in.shape[0]
    b_p, seq, d = x_prompt.shape
    n_meta = meta_tokens.shape[0]
    sb_w = w_branch.shape[2]
    rw_cols = rw_mu.shape[1]
    gdn_heads, gdn_dim = gdn_a_log.shape[1], gdn_norm_w.shape[1]
    gdn_w = gdn_heads * gdn_dim
    rw_heads, rw_dim = rw_r_k.shape[1], rw_r_k.shape[2]
    gdn_taps = gdn_conv_w.shape[1]
    off_rw = 3 * sb_w
    off_gdn = off_rw + rw_cols
    off_ab = off_gdn + 3 * gdn_w
    off_z = off_ab + 2 * gdn_heads
    off_gate = off_z + gdn_w

    t_p = n_meta + seq
    tp_pad = _round_up(t_p, 256)
    hp = jnp.concatenate([jnp.broadcast_to(meta_tokens.astype(x_prompt.dtype)[None], (b_p, n_meta, d)), x_prompt],
                         axis=1)
    hp = jnp.pad(hp, ((0, 0), (0, tp_pad - t_p), (0, 0)))
    hs = x_sample
    t_s = x_sample.shape[1]

    p_rows, s_rows = [], []
    for l in range(depth):
        wl = w_in[l]
        lp = dict(
            norm_mix=norm_mix[l], norm_ffn=norm_ffn[l],
            rw_mu=rw_mu[l], rw_w0=rw_w0[l], rw_w2=rw_w2[l], rw_a0=rw_a0[l], rw_a2=rw_a2[l], rw_g2=rw_g2[l],
            rw_k_k=rw_k_k[l], rw_k_a=rw_k_a[l], rw_r_k=rw_r_k[l], rw_ln_w=rw_ln_w[l], rw_ln_b=rw_ln_b[l],
            gdn_conv_w=gdn_conv_w[l], gdn_a_log=gdn_a_log[l], gdn_dt_bias=gdn_dt_bias[l], gdn_norm_w=gdn_norm_w[l],
            w_branch=w_branch[l].astype(BF16), w_out=w_out[l].astype(BF16),
            peer_wq_t=peer_wq[l].T.astype(BF16), peer_subkeys=peer_subkeys[l],
            peer_u=peer_u[l].astype(BF16), peer_v=peer_v[l].astype(BF16))
        lp.update(
            w_sb=wl[:, :off_rw].astype(BF16),
            w_rw=_pad_segments(wl[:, off_rw:off_gdn], _rw_segments(lp)).astype(BF16),
            w_gdn=wl[:, off_gdn:off_ab].astype(BF16),
            w_ab=_pad_segments(wl[:, off_ab:off_z], [(gdn_heads, LANE)] * 2).astype(BF16),
            w_z=wl[:, off_z:off_gate].astype(BF16),
            w_gate=wl[:, off_gate:].astype(BF16))
        hp, p_new = _layer(
            hp, lp, t_p, None, None,
            jnp.zeros((b_p, 1, rw_cols), F32), jnp.zeros((b_p, rw_heads, rw_dim, rw_dim), F32),
            jnp.zeros((b_p, gdn_taps - 1, 3 * gdn_w), F32), jnp.zeros((b_p, gdn_heads, gdn_dim, gdn_dim), F32),
            True)
        hs, s_new = _layer(
            hs, lp, t_s, cache_sb_k[l], cache_sb_v[l], state_rw_shift[l], state_rw[l],
            state_gdn_conv[l], state_gdn[l], False)
        p_rows.append(p_new)
        s_rows.append(s_new)

    p_out = [jnp.stack(z) for z in zip(*p_rows)]
    s_out = [jnp.stack(z) for z in zip(*s_rows)]
    y_prompt = _rmsnorm(hp.reshape(b_p * tp_pad, d), norm_final, F32).reshape(b_p, tp_pad, d)[:, n_meta:t_p]
    b_s = hs.shape[0]
    y_sample = _rmsnorm(hs.reshape(b_s * t_s, d), norm_final, F32).reshape(b_s, t_s, d)
    return (y_prompt, y_sample, *p_out, *s_out)
```
